```python
import math
import jax, jax.numpy as jnp
from jax import lax
import numpy as np

D_MODEL = 1024
BATCH = 8
SEQ = 2048
DEPTH = 4
DEC_BATCH = 128
DEC_SEQ = 8
PAST_LEN = 8192
PAGE_SIZE = 128

N_MIXERS = 3
N_FOX = (DEPTH + 2) // 3
N_SWA = (DEPTH + 1) // 3
N_SSM = DEPTH // 3
FOX_HEADS = 16
FOX_KV_HEADS = 4
FOX_HEAD_DIM = 64
FOX_Q_BLOCK = 128
F_BIAS_INIT = 7.0
SWA_HEADS = 16
SWA_KV_HEADS = 2
SWA_HEAD_DIM = 64
WINDOW = 128
ROPE_THETA = 10000.0
SSM_GROUP = 16
SSM_GROUPS = D_MODEL // SSM_GROUP
SSM_STATE = 64
D_FF = 2816
N_MEM = 256
X_HEADS = 4
X_HEAD_DIM = D_MODEL // 8
RMS_EPS = 1e-6
NEG_INF = -1e30

kernel_name = 'hybrid_fox_swa_s5_step'


def _rmsnorm(x, g):
    xf = x.astype(jnp.float32)
    y = xf * lax.rsqrt(jnp.mean(xf * xf, axis=-1, keepdims=True) + RMS_EPS)
    return (y * g.astype(jnp.float32)).astype(x.dtype)


def _swiglu(h, wg, wu, wd):
    return (jax.nn.silu(h @ wg) * (h @ wu)) @ wd


def _rope(x, pos):
    half = x.shape[-1] // 2
    inv_freq = ROPE_THETA ** (-jnp.arange(half, dtype=jnp.float32) / half)
    ang = pos.astype(jnp.float32)[:, None] * inv_freq[None, :]
    cos = jnp.cos(ang)[:, None, :]
    sin = jnp.sin(ang)[:, None, :]
    xf = x.astype(jnp.float32)
    x1, x2 = xf[..., :half], xf[..., half:]
    return jnp.concatenate([x1 * cos - x2 * sin, x2 * cos + x1 * sin], axis=-1).astype(x.dtype)


def _attend(q, k, v, bias=None, sink=None):
    *lead, t, h, hd = q.shape
    kvh = k.shape[-2]
    g = h // kvh
    qg = q.reshape(*lead, t, kvh, g, hd)
    s = jnp.einsum('...tkgd,...skd->...kgts', qg, k).astype(jnp.float32) * (hd ** -0.5)
    if bias is not None:
        s = s + bias
    if sink is not None:
        sk = jnp.broadcast_to(sink.astype(jnp.float32).reshape(kvh, g, 1, 1), s.shape[:-1] + (1,))
        p = jax.nn.softmax(jnp.concatenate([s, sk], axis=-1), axis=-1)[..., :-1]
    else:
        p = jax.nn.softmax(s, axis=-1)
    o = jnp.einsum('...kgts,...skd->...tkgd', p.astype(v.dtype), v)
    return o.reshape(*lead, t, h * hd)


def _heads_to_groups(c):
    b, l, _ = c.shape
    return c.transpose(0, 2, 1).reshape(b, FOX_KV_HEADS, FOX_HEADS // FOX_KV_HEADS, l)


def _fox_project(h, w_qkvf, b_f):
    b, l, _ = h.shape
    qd = FOX_HEADS * FOX_HEAD_DIM
    kd = FOX_KV_HEADS * FOX_HEAD_DIM
    proj = h @ w_qkvf
    q = proj[..., :qd].reshape(b, l, FOX_HEADS, FOX_HEAD_DIM)
    k = proj[..., qd:qd + kd].reshape(b, l, FOX_KV_HEADS, FOX_HEAD_DIM)
    v = proj[..., qd + kd:qd + 2 * kd].reshape(b, l, FOX_KV_HEADS, FOX_HEAD_DIM)
    logf = jax.nn.log_sigmoid((proj[..., qd + 2 * kd:] + b_f).astype(jnp.float32))
    return q, k, v, logf


def _fox_prompt(h, w_qkvf, b_f, w_o):
    b, l, _ = h.shape
    q, k, v, logf = _fox_project(h, w_qkvf, b_f)
    c = _heads_to_groups(lax.cumsum(logf, axis=1))
    nb = l // FOX_Q_BLOCK
    qb = jnp.moveaxis(q.reshape(b, nb, FOX_Q_BLOCK, FOX_HEADS, FOX_HEAD_DIM), 1, 0)
    cqb = jnp.moveaxis(c.reshape(b, FOX_KV_HEADS, FOX_HEADS // FOX_KV_HEADS, nb, FOX_Q_BLOCK), 3, 0)
    kpos = jnp.arange(l)

    def block(args):
        q_i, cq_i, i = args
        qpos = i * FOX_Q_BLOCK + jnp.arange(FOX_Q_BLOCK)
        bias = cq_i[..., :, None] - c[..., None, :]
        bias = jnp.where(kpos[None, :] <= qpos[:, None], bias, NEG_INF)
        return _attend(q_i, k, v, bias)

    o = lax.map(block, (qb, cqb, jnp.arange(nb)))
    o = jnp.moveaxis(o, 0, 1).reshape(b, l, FOX_HEADS * FOX_HEAD_DIM)
    return o @ w_o, k, v, logf.astype(h.dtype)


def _fox_sample(h, k_pool, v_pool, f_pool, page_table, w_qkvf, b_f, w_o):
    b, t, _ = h.shape
    q, k_new, v_new, logf_new = _fox_project(h, w_qkvf, b_f)
    past = page_table.shape[1] * k_pool.shape[1]
    k_past = k_pool[page_table].reshape(b, past, FOX_KV_HEADS, FOX_HEAD_DIM).astype(k_new.dtype)
    v_past = v_pool[page_table].reshape(b, past, FOX_KV_HEADS, FOX_HEAD_DIM).astype(v_new.dtype)
    f_past = f_pool[page_table].reshape(b, past, FOX_HEADS).astype(jnp.float32)
    suffix = lax.cumsum(f_past, axis=1, reverse=True) - f_past
    c_new = lax.cumsum(logf_new, axis=1)
    cq = _heads_to_groups(c_new)
    ck = _heads_to_groups(jnp.concatenate([-suffix, c_new], axis=1))
    bias = cq[..., :, None] - ck[..., None, :]
    kidx = jnp.arange(past + t)
    qidx = past + jnp.arange(t)
    bias = jnp.where(kidx[None, :] <= qidx[:, None], bias, NEG_INF)
    keys = jnp.concatenate([k_past, k_new], axis=1)
    vals = jnp.concatenate([v_past, v_new], axis=1)
    o = _attend(q, keys, vals, bias)
    return o @ w_o, k_new, v_new, logf_new.astype(h.dtype)


def _swa_project(h, pos, w_qkv):
    b, l, _ = h.shape
    qd = SWA_HEADS * SWA_HEAD_DIM
    kd = SWA_KV_HEADS * SWA_HEAD_DIM
    proj = h @ w_qkv
    q = _rope(proj[..., :qd].reshape(b, l, SWA_HEADS, SWA_HEAD_DIM), pos)
    k = _rope(proj[..., qd:qd + kd].reshape(b, l, SWA_KV_HEADS, SWA_HEAD_DIM), pos)
    v = proj[..., qd + kd:].reshape(b, l, SWA_KV_HEADS, SWA_HEAD_DIM)
    return q, k, v


def _swa_prompt(h, pos, w_qkv, sink, w_o):
    b, l, _ = h.shape
    q, k, v = _swa_project(h, pos, w_qkv)
    nb = l // WINDOW
    qb = q.reshape(b, nb, WINDOW, SWA_HEADS, SWA_HEAD_DIM)

    def with_prev(x):
        xb = x.reshape(b, nb, WINDOW, SWA_KV_HEADS, SWA_HEAD_DIM)
        prev = jnp.concatenate([jnp.zeros_like(xb[:, :1]), xb[:, :-1]], axis=1)
        return jnp.concatenate([prev, xb], axis=2)

    kk, vv = with_prev(k), with_prev(v)
    qi = jnp.arange(WINDOW)[:, None]
    kj = jnp.arange(2 * WINDOW)[None, :]
    dist = WINDOW + qi - kj
    band = (dist >= 0) & (dist <= WINDOW)
    real = (jnp.arange(nb)[:, None, None] > 0) | (kj[None] >= WINDOW)
    bias = jnp.where(band[None] & real, 0.0, NEG_INF)[None, :, None, None]
    o = _attend(qb, kk, vv, bias, sink).reshape(b, l, SWA_HEADS * SWA_HEAD_DIM)
    n_keep = min(WINDOW, l)
    return o @ w_o, k[:, l - n_keep:], v[:, l - n_keep:]


def _swa_sample(h, pos, past_len, k_buf, v_buf, w_qkv, sink, w_o):
    b, t, _ = h.shape
    q, k, v = _swa_project(h, pos, w_qkv)
    n_buf = k_buf.shape[1]
    kk = jnp.concatenate([k_buf.astype(k.dtype), k], axis=1)
    vv = jnp.concatenate([v_buf.astype(v.dtype), v], axis=1)
    kpos = jnp.concatenate([past_len - n_buf + jnp.arange(n_buf), pos])
    dist = pos[:, None] - kpos[None, :]
    bias = jnp.where((dist >= 0) & (dist <= WINDOW), 0.0, NEG_INF)
    o = _attend(q, kk, vv, bias, sink).reshape(b, t, SWA_HEADS * SWA_HEAD_DIM)
    return o @ w_o, kk[:, -n_buf:], vv[:, -n_buf:]


def _s5(h, h0, a_re, a_im, b_re, b_im, c_re, c_im, d_skip, log_dt, w_glu):
    bsz, l, dm = h.shape
    f32 = jnp.float32
    u = h.astype(f32).reshape(bsz, l, SSM_GROUPS, SSM_GROUP)
    a = lax.complex(a_re.astype(f32), a_im.astype(f32))
    dt = jnp.exp(log_dt.astype(f32))[:, None]
    a_bar = jnp.exp(a * dt)
    bmat = lax.complex(b_re.astype(f32), b_im.astype(f32))
    b_bar = ((a_bar - 1.0) / a)[..., None] * bmat
    bu = jnp.einsum('blgc,gpc->blgp', u.astype(jnp.complex64), b_bar)
    bu = bu.at[:, 0].add(a_bar * h0)
    a_seq = jnp.broadcast_to(a_bar, bu.shape)

    def combine(e1, e2):
        a1, x1 = e1
        a2, x2 = e2
        return a1 * a2, a2 * x1 + x2

    _, states = lax.associative_scan(combine, (a_seq, bu), axis=1)
    y = (jnp.einsum('blgp,gcp->blgc', states.real, c_re.astype(f32))
         - jnp.einsum('blgp,gcp->blgc', states.imag, c_im.astype(f32)))
    y = y.reshape(bsz, l, dm) + d_skip.astype(f32) * h.astype(f32)
    z = jax.nn.gelu(y).astype(h.dtype)
    zw = z @ w_glu
    out = zw[..., :dm] * jax.nn.sigmoid(zw[..., dm:])
    last = states[:, -1]
    return out, last.real.astype(h.dtype), last.imag.astype(h.dtype)


def _mem_kv(mem, g, w_xkv):
    b, m, _ = mem.shape
    xd = X_HEADS * X_HEAD_DIM
    kv = _rmsnorm(mem, g) @ w_xkv
    return (kv[..., :xd].reshape(b, m, X_HEADS, X_HEAD_DIM),
            kv[..., xd:].reshape(b, m, X_HEADS, X_HEAD_DIM))


def _cross(h, mk, mv, w_xq, w_xo):
    b, t, _ = h.shape
    q = (h @ w_xq).reshape(b, t, X_HEADS, X_HEAD_DIM)
    return _attend(q, mk.astype(q.dtype), mv.astype(q.dtype)) @ w_xo


def _stack(entries, i):
    return jnp.stack([e[i] for e in entries])


def setup_inputs(seed: int = 0) -> dict:
    key = jax.random.key(seed)
    keys = iter(jax.random.split(key, 64))
    f32 = jnp.float32

    def nrm(shape, scale=1.0):
        return scale * jax.random.normal(next(keys), shape, f32)

    n_pages = PAST_LEN // PAGE_SIZE
    n_used = DEC_BATCH * n_pages
    n_pool = n_used + max(1, n_used // 4)
    page_table = jax.random.permutation(next(keys), n_pool)[:n_used].reshape(DEC_BATCH, n_pages).astype(jnp.int32)
    fox_proj = FOX_HEADS * FOX_HEAD_DIM + 2 * FOX_KV_HEADS * FOX_HEAD_DIM + FOX_HEADS
    swa_proj = SWA_HEADS * SWA_HEAD_DIM + 2 * SWA_KV_HEADS * SWA_HEAD_DIM
    xd = X_HEADS * X_HEAD_DIM
    fox_o_in = FOX_HEADS * FOX_HEAD_DIM
    swa_o_in = SWA_HEADS * SWA_HEAD_DIM
    inv_d = D_MODEL ** -0.5
    n_idx = jnp.arange(SSM_STATE, dtype=f32)
    return {
        'x_prompt': nrm((BATCH, SEQ, D_MODEL)),
        'x_sample': nrm((DEC_BATCH, DEC_SEQ, D_MODEL)),
        'mem_prompt': nrm((BATCH, N_MEM, D_MODEL)),
        'cache_fox_k': nrm((N_FOX, n_pool, PAGE_SIZE, FOX_KV_HEADS, FOX_HEAD_DIM)),
        'cache_fox_v': nrm((N_FOX, n_pool, PAGE_SIZE, FOX_KV_HEADS, FOX_HEAD_DIM)),
        'cache_fox_logf': jax.nn.log_sigmoid(F_BIAS_INIT + nrm((N_FOX, n_pool, PAGE_SIZE, FOX_HEADS), 0.5)),
        'state_swa_k': nrm((N_SWA, DEC_BATCH, min(WINDOW, PAST_LEN), SWA_KV_HEADS, SWA_HEAD_DIM)),
        'state_swa_v': nrm((N_SWA, DEC_BATCH, min(WINDOW, PAST_LEN), SWA_KV_HEADS, SWA_HEAD_DIM)),
        'state_ssm_re': nrm((N_SSM, DEC_BATCH, SSM_GROUPS, SSM_STATE), 0.3),
        'state_ssm_im': nrm((N_SSM, DEC_BATCH, SSM_GROUPS, SSM_STATE), 0.3),
        'cache_mem_k': nrm((DEPTH, DEC_BATCH, N_MEM, X_HEADS, X_HEAD_DIM)),
        'cache_mem_v': nrm((DEPTH, DEC_BATCH, N_MEM, X_HEADS, X_HEAD_DIM)),
        'page_table': page_table,
        'norm_g': 1.0 + nrm((DEPTH, 5, D_MODEL), 0.05),
        'final_g': 1.0 + nrm((D_MODEL,), 0.05),
        'w_ffn_gate': nrm((DEPTH, 2, D_MODEL, D_FF), inv_d),
        'w_ffn_up': nrm((DEPTH, 2, D_MODEL, D_FF), inv_d),
        'w_ffn_down': nrm((DEPTH, 2, D_FF, D_MODEL), D_FF ** -0.5),
        'w_xq': nrm((DEPTH, D_MODEL, xd), inv_d),
        'w_xkv': nrm((DEPTH, D_MODEL, 2 * xd), inv_d),
        'w_xo': nrm((DEPTH, xd, D_MODEL), xd ** -0.5),
        'w_fox_qkvf': nrm((N_FOX, D_MODEL, fox_proj), inv_d),
        'b_fox_f': F_BIAS_INIT + nrm((N_FOX, FOX_HEADS), 0.5),
        'w_fox_o': nrm((N_FOX, fox_o_in, D_MODEL), fox_o_in ** -0.5),
        'w_swa_qkv': nrm((N_SWA, D_MODEL, swa_proj), inv_d),
        'swa_sink': nrm((N_SWA, SWA_HEADS)),
        'w_swa_o': nrm((N_SWA, swa_o_in, D_MODEL), swa_o_in ** -0.5),
        'ssm_a_re': -0.5 + nrm((N_SSM, SSM_GROUPS, SSM_STATE), 0.01),
        'ssm_a_im': math.pi * n_idx + nrm((N_SSM, SSM_GROUPS, SSM_STATE), 0.01),
        'ssm_b_re': nrm((N_SSM, SSM_GROUPS, SSM_STATE, SSM_GROUP), (2 * SSM_GROUP) ** -0.5),
        'ssm_b_im': nrm((N_SSM, SSM_GROUPS, SSM_STATE, SSM_GROUP), (2 * SSM_GROUP) ** -0.5),
        'ssm_c_re': nrm((N_SSM, SSM_GROUPS, SSM_GROUP, SSM_STATE), SSM_STATE ** -0.5),
        'ssm_c_im': nrm((N_SSM, SSM_GROUPS, SSM_GROUP, SSM_STATE), SSM_STATE ** -0.5),
        'ssm_d': nrm((N_SSM, D_MODEL)),
        'ssm_log_dt': jax.random.uniform(next(keys), (N_SSM, SSM_GROUPS), f32, math.log(1e-3), math.log(1e-1)),
        'w_ssm_glu': nrm((N_SSM, D_MODEL, 2 * D_MODEL), inv_d),
    }


def reference(x_prompt, x_sample, mem_prompt, cache_fox_k, cache_fox_v, cache_fox_logf,
              state_swa_k, state_swa_v, state_ssm_re, state_ssm_im, cache_mem_k, cache_mem_v,
              page_table, norm_g, final_g, w_ffn_gate, w_ffn_up, w_ffn_down,
              w_xq, w_xkv, w_xo, w_fox_qkvf, b_fox_f, w_fox_o, w_swa_qkv, swa_sink, w_swa_o,
              ssm_a_re, ssm_a_im, ssm_b_re, ssm_b_im, ssm_c_re, ssm_c_im, ssm_d, ssm_log_dt,
              w_ssm_glu):
    past_len = page_table.shape[1] * cache_fox_k.shape[2]
    pos_p = jnp.arange(x_prompt.shape[1])
    pos_s = past_len + jnp.arange(x_sample.shape[1])
    yp, ys = x_prompt, x_sample
    fox_p, fox_s, swa_p, swa_s, ssm_p, ssm_s, mem_p = [], [], [], [], [], [], []
    for l in range(DEPTH):
        kind, j = l % N_MIXERS, l // N_MIXERS
        yp = yp + 0.5 * _swiglu(_rmsnorm(yp, norm_g[l, 0]), w_ffn_gate[l, 0], w_ffn_up[l, 0], w_ffn_down[l, 0])
        ys = ys + 0.5 * _swiglu(_rmsnorm(ys, norm_g[l, 0]), w_ffn_gate[l, 0], w_ffn_up[l, 0], w_ffn_down[l, 0])
        hp = _rmsnorm(yp, norm_g[l, 1])
        hs = _rmsnorm(ys, norm_g[l, 1])
        if kind == 0:
            op, kp, vp, fp = _fox_prompt(hp, w_fox_qkvf[j], b_fox_f[j], w_fox_o[j])
            osm, kn, vn, fn = _fox_sample(hs, cache_fox_k[j], cache_fox_v[j], cache_fox_logf[j], page_table,
                                          w_fox_qkvf[j], b_fox_f[j], w_fox_o[j])
            fox_p.append((kp, vp, fp))
            fox_s.append((kn, vn, fn))
        elif kind == 1:
            op, kp, vp = _swa_prompt(hp, pos_p, w_swa_qkv[j], swa_sink[j], w_swa_o[j])
            osm, kn, vn = _swa_sample(hs, pos_s, past_len, state_swa_k[j], state_swa_v[j],
                                      w_swa_qkv[j], swa_sink[j], w_swa_o[j])
            swa_p.append((kp, vp))
            swa_s.append((kn, vn))
        else:
            h0p = jnp.zeros((hp.shape[0], SSM_GROUPS, SSM_STATE), jnp.complex64)
            h0s = lax.complex(state_ssm_re[j].astype(jnp.float32), state_ssm_im[j].astype(jnp.float32))
            op, rp, ip = _s5(hp, h0p, ssm_a_re[j], ssm_a_im[j], ssm_b_re[j], ssm_b_im[j],
                             ssm_c_re[j], ssm_c_im[j], ssm_d[j], ssm_log_dt[j], w_ssm_glu[j])
            osm, rn, inn = _s5(hs, h0s, ssm_a_re[j], ssm_a_im[j], ssm_b_re[j], ssm_b_im[j],
                               ssm_c_re[j], ssm_c_im[j], ssm_d[j], ssm_log_dt[j], w_ssm_glu[j])
            ssm_p.append((rp, ip))
            ssm_s.append((rn, inn))
        yp = yp + op
        ys = ys + osm
        mk, mv = _mem_kv(mem_prompt, norm_g[l, 2], w_xkv[l])
        mem_p.append((mk, mv))
        yp = yp + _cross(_rmsnorm(yp, norm_g[l, 3]), mk, mv, w_xq[l], w_xo[l])
        ys = ys + _cross(_rmsnorm(ys, norm_g[l, 3]), cache_mem_k[l], cache_mem_v[l], w_xq[l], w_xo[l])
        yp = yp + 0.5 * _swiglu(_rmsnorm(yp, norm_g[l, 4]), w_ffn_gate[l, 1], w_ffn_up[l, 1], w_ffn_down[l, 1])
        ys = ys + 0.5 * _swiglu(_rmsnorm(ys, norm_g[l, 4]), w_ffn_gate[l, 1], w_ffn_up[l, 1], w_ffn_down[l, 1])
    y_prompt = _rmsnorm(yp, final_g)
    y_sample = _rmsnorm(ys, final_g)
    fox_k_prompt, fox_v_prompt, fox_logf_prompt = _stack(fox_p, 0), _stack(fox_p, 1), _stack(fox_p, 2)
    fox_k_sample, fox_v_sample, fox_logf_sample = _stack(fox_s, 0), _stack(fox_s, 1), _stack(fox_s, 2)
    swa_k_prompt, swa_v_prompt = _stack(swa_p, 0), _stack(swa_p, 1)
    swa_k_sample, swa_v_sample = _stack(swa_s, 0), _stack(swa_s, 1)
    ssm_re_prompt, ssm_im_prompt = _stack(ssm_p, 0), _stack(ssm_p, 1)
    ssm_re_sample, ssm_im_sample = _stack(ssm_s, 0), _stack(ssm_s, 1)
    mem_k_prompt, mem_v_prompt = _stack(mem_p, 0), _stack(mem_p, 1)
    return (y_prompt, y_sample,
            fox_k_prompt, fox_v_prompt, fox_logf_prompt,
            swa_k_prompt, swa_v_prompt,
            ssm_re_prompt, ssm_im_prompt,
            mem_k_prompt, mem_v_prompt,
            fox_k_sample, fox_v_sample, fox_logf_sample,
            swa_k_sample, swa_v_sample,
            ssm_re_sample, ssm_im_sample)
```

```python
import functools
import math

import jax
import jax.numpy as jnp
from jax import lax
from jax.experimental import pallas as pl
from jax.experimental.pallas import tpu as pltpu

F32 = jnp.float32
BF16 = jnp.bfloat16

RMS_EPS = 1e-6
NEG_INF = -1e30
ROPE_THETA = 10000.0
SSM_GROUP = 16
GELU_C = math.sqrt(2.0 / math.pi)

V7X_VMEM_BYTES = 64 * 1024 * 1024
LANES = 128
SUBLANES = 8


def _cparams(sem, vmem_mb):
    assert vmem_mb * 2**20 < V7X_VMEM_BYTES
    return pltpu.CompilerParams(dimension_semantics=sem, vmem_limit_bytes=vmem_mb * 2**20)


def _rms(x, g):
    ms = jnp.mean(x * x, axis=-1, keepdims=True)
    return (x * lax.rsqrt(ms + RMS_EPS)) * g


def _dot(a, b):
    return jnp.dot(a, b, preferred_element_type=F32)


def _dot_nt(a, b):
    return lax.dot_general(a, b, (((1,), (1,)), ((), ())), preferred_element_type=F32)


def _dot_exact(a, b):
    return jnp.dot(a, b, preferred_element_type=F32, precision=lax.Precision.HIGHEST)


def _row_tile(m, want):
    t = min(want, m)
    assert m % t == 0, (m, t)
    return t


def _ffn_kernel(x_ref, g_ref, wg_ref, wu_ref, wd_ref, o_ref, h_ref, acc_ref):
    j = pl.program_id(1)

    @pl.when(j == 0)
    def _():
        h_ref[...] = _rms(x_ref[...], g_ref[...]).astype(BF16)
        acc_ref[...] = jnp.zeros_like(acc_ref)

    h = h_ref[...]
    gate = _dot(h, wg_ref[...])
    up = _dot(h, wu_ref[...])
    act = (gate * jax.nn.sigmoid(gate) * up).astype(BF16)
    acc_ref[...] += _dot(act, wd_ref[...])

    @pl.when(j == pl.num_programs(1) - 1)
    def _():
        o_ref[...] = x_ref[...] + 0.5 * acc_ref[...]


def _ffn(x, g, wg, wu, wd):
    m, d = x.shape
    ff = wg.shape[1]
    tm = _row_tile(m, 1024)
    tf = 256
    assert ff % tf == 0
    return pl.pallas_call(
        _ffn_kernel,
        grid=(m // tm, ff // tf),
        in_specs=[
            pl.BlockSpec((tm, d), lambda i, j: (i, 0)),
            pl.BlockSpec((1, d), lambda i, j: (0, 0)),
            pl.BlockSpec((d, tf), lambda i, j: (0, j)),
            pl.BlockSpec((d, tf), lambda i, j: (0, j)),
            pl.BlockSpec((tf, d), lambda i, j: (j, 0)),
        ],
        out_specs=pl.BlockSpec((tm, d), lambda i, j: (i, 0)),
        out_shape=jax.ShapeDtypeStruct((m, d), F32),
        scratch_shapes=[pltpu.VMEM((tm, d), BF16), pltpu.VMEM((tm, d), F32)],
        compiler_params=_cparams(("arbitrary", "arbitrary"), 48),
        name="ffn",
    )(x, g.reshape(1, d), wg, wu, wd)


def _nmm_kernel(x_ref, g_ref, w_ref, o_ref):
    h = _rms(x_ref[...], g_ref[...]).astype(BF16)
    o_ref[...] = _dot(h, w_ref[...]).astype(o_ref.dtype)


def _norm_matmul(x, g, w, out_dtype):
    m, d = x.shape
    n = w.shape[1]
    tm = _row_tile(m, 512)
    return pl.pallas_call(
        _nmm_kernel,
        grid=(m // tm,),
        in_specs=[
            pl.BlockSpec((tm, d), lambda i: (i, 0)),
            pl.BlockSpec((1, d), lambda i: (0, 0)),
            pl.BlockSpec((d, n), lambda i: (0, 0)),
        ],
        out_specs=pl.BlockSpec((tm, n), lambda i: (i, 0)),
        out_shape=jax.ShapeDtypeStruct((m, n), out_dtype),
        compiler_params=_cparams(("arbitrary",), 40),
        name="norm_matmul",
    )(x, g.reshape(1, d), w)


def _mmres_kernel(a_ref, w_ref, x_ref, o_ref):
    o_ref[...] = x_ref[...] + _dot(a_ref[...].astype(BF16), w_ref[...])


def _matmul_residual(a, w, x):
    m, k = a.shape
    d = w.shape[1]
    tm = _row_tile(m, 512)
    return pl.pallas_call(
        _mmres_kernel,
        grid=(m // tm,),
        in_specs=[
            pl.BlockSpec((tm, k), lambda i: (i, 0)),
            pl.BlockSpec((k, d), lambda i: (0, 0)),
            pl.BlockSpec((tm, d), lambda i: (i, 0)),
        ],
        out_specs=pl.BlockSpec((tm, d), lambda i: (i, 0)),
        out_shape=jax.ShapeDtypeStruct((m, d), F32),
        compiler_params=_cparams(("arbitrary",), 40),
        name="matmul_residual",
    )(a, w, x)


def _final_norm_kernel(x_ref, g_ref, o_ref):
    o_ref[...] = _rms(x_ref[...], g_ref[...])


def _final_norm(x, g):
    m, d = x.shape
    tm = _row_tile(m, 1024)
    return pl.pallas_call(
        _final_norm_kernel,
        grid=(m // tm,),
        in_specs=[pl.BlockSpec((tm, d), lambda i: (i, 0)), pl.BlockSpec((1, d), lambda i: (0, 0))],
        out_specs=pl.BlockSpec((tm, d), lambda i: (i, 0)),
        out_shape=jax.ShapeDtypeStruct((m, d), F32),
        compiler_params=_cparams(("arbitrary",), 40),
        name="final_norm",
    )(x, g.reshape(1, d))


def _fox_proj_kernel(x_ref, g_ref, w_ref, b_ref, q_ref, k_ref, v_ref, lf_ref, c_ref, carry_ref,
                     *, qd, kd, nh, tiles_per_seq, q_scale):
    i = pl.program_id(0)
    h = _rms(x_ref[...], g_ref[...]).astype(BF16)
    proj = _dot(h, w_ref[...])
    q_ref[...] = (proj[:, :qd] * q_scale).astype(q_ref.dtype)
    k_ref[...] = proj[:, qd:qd + kd]
    v_ref[...] = proj[:, qd + kd:qd + 2 * kd]
    z = proj[:, qd + 2 * kd:] + b_ref[...]
    lf = jnp.minimum(z, 0.0) - jnp.log1p(jnp.exp(-jnp.abs(z)))
    lf_ref[...] = lf[:, :nh]

    @pl.when(i % tiles_per_seq == 0)
    def _():
        carry_ref[...] = jnp.zeros_like(carry_ref)

    tm = lf.shape[0]
    r = lax.broadcasted_iota(jnp.int32, (tm, tm), 0)
    c = lax.broadcasted_iota(jnp.int32, (tm, tm), 1)
    tril = jnp.where(c <= r, 1.0, 0.0).astype(F32)
    csum = _dot_exact(tril, lf) + carry_ref[...]
    c_ref[...] = csum[:, :nh]
    carry_ref[...] = csum[tm - 1:tm, :]


def _fox_project(x, g, w_pad, b_pad, seq_len, *, qd, kd, nh, hd, q_dtype):
    m, d = x.shape
    n = w_pad.shape[1]
    tm = _row_tile(seq_len, 512)
    kern = functools.partial(_fox_proj_kernel, qd=qd, kd=kd, nh=nh, tiles_per_seq=seq_len // tm,
                             q_scale=hd ** -0.5)
    return pl.pallas_call(
        kern,
        grid=(m // tm,),
        in_specs=[
            pl.BlockSpec((tm, d), lambda i: (i, 0)),
            pl.BlockSpec((1, d), lambda i: (0, 0)),
            pl.BlockSpec((d, n), lambda i: (0, 0)),
            pl.BlockSpec((1, LANES), lambda i: (0, 0)),
        ],
        out_specs=[
            pl.BlockSpec((tm, qd), lambda i: (i, 0)),
            pl.BlockSpec((tm, kd), lambda i: (i, 0)),
            pl.BlockSpec((tm, kd), lambda i: (i, 0)),
            pl.BlockSpec((tm, nh), lambda i: (i, 0)),
            pl.BlockSpec((tm, nh), lambda i: (i, 0)),
        ],
        out_shape=[
            jax.ShapeDtypeStruct((m, qd), q_dtype),
            jax.ShapeDtypeStruct((m, kd), F32),
            jax.ShapeDtypeStruct((m, kd), F32),
            jax.ShapeDtypeStruct((m, nh), F32),
            jax.ShapeDtypeStruct((m, nh), F32),
        ],
        scratch_shapes=[pltpu.VMEM((1, LANES), F32)],
        compiler_params=_cparams(("arbitrary",), 48),
        name="fox_project",
    )(x, g.reshape(1, d), w_pad, b_pad)


def _fox_prompt_kernel(q_ref, k_ref, v_ref, c_ref, ct_ref, o_ref, *, tq, group, kv_heads):
    kv = pl.program_id(1)
    i = pl.program_id(2)
    q = q_ref[0, 0, 0]
    c_all = c_ref[...]
    rows = group * tq

    def head_col(gi):
        cols = [c_all[:, kk * group + gi:kk * group + gi + 1] for kk in range(kv_heads)]
        out = cols[0]
        for kk in range(1, kv_heads):
            out = jnp.where(kv == kk, cols[kk], out)
        return out

    cq = jnp.concatenate([head_col(gi) for gi in range(group)], axis=0)

    def ck_block(j):
        parts = []
        for gi in range(group):
            row = ct_ref[0, 0, gi, pl.ds(j, 1), :]
            parts.append(jnp.broadcast_to(row, (tq, tq)))
        return jnp.concatenate(parts, axis=0)

    def step(j, carry, masked):
        m, l, acc = carry
        start = pl.multiple_of(j * tq, tq)
        k_j = k_ref[0, 0, pl.ds(start, tq), :]
        v_j = v_ref[0, 0, pl.ds(start, tq), :]
        s = _dot_nt(q, k_j) + (cq - ck_block(j))
        if masked:
            r = lax.broadcasted_iota(jnp.int32, (rows, tq), 0) % tq
            cc = lax.broadcasted_iota(jnp.int32, (rows, tq), 1)
            s = jnp.where(cc <= r, s, NEG_INF)
        m_new = jnp.maximum(m, jnp.max(s, axis=1, keepdims=True))
        alpha = jnp.exp(m - m_new)
        p = jnp.exp(s - m_new)
        l = alpha * l + jnp.sum(p, axis=1, keepdims=True)
        acc = alpha * acc + _dot(p.astype(BF16), v_j)
        return m_new, l, acc

    hd = q.shape[1]
    init = (jnp.full((rows, 1), NEG_INF, F32), jnp.zeros((rows, 1), F32), jnp.zeros((rows, hd), F32))
    carry = lax.fori_loop(0, i, lambda j, cr: step(j, cr, False), init)
    m, l, acc = step(i, carry, True)
    o_ref[0, 0, 0] = (acc / l).astype(o_ref.dtype)


def _fox_prompt_attention(q, k, v, c, batch, seq_len, n_heads, kv_heads, hd):
    group = n_heads // kv_heads
    tq = _row_tile(seq_len, 256)
    nq = seq_len // tq
    q_r = (q.reshape(batch, nq, tq, kv_heads, group, hd).transpose(0, 3, 1, 4, 2, 5)
           .reshape(batch, kv_heads, nq, group * tq, hd))
    k_r = k.astype(BF16).reshape(batch, seq_len, kv_heads, hd).transpose(0, 2, 1, 3)
    v_r = v.astype(BF16).reshape(batch, seq_len, kv_heads, hd).transpose(0, 2, 1, 3)
    ct = (c.reshape(batch, seq_len, kv_heads, group).transpose(0, 2, 3, 1)
          .reshape(batch, kv_heads, group, nq, tq))
    kern = functools.partial(_fox_prompt_kernel, tq=tq, group=group, kv_heads=kv_heads)
    o = pl.pallas_call(
        kern,
        grid=(batch, kv_heads, nq),
        in_specs=[
            pl.BlockSpec((1, 1, 1, group * tq, hd), lambda b, kv, i: (b, kv, i, 0, 0)),
            pl.BlockSpec((1, 1, seq_len, hd), lambda b, kv, i: (b, kv, 0, 0)),
            pl.BlockSpec((1, 1, seq_len, hd), lambda b, kv, i: (b, kv, 0, 0)),
            pl.BlockSpec((tq, n_heads), lambda b, kv, i: (b * nq + i, 0)),
            pl.BlockSpec((1, 1, group, nq, tq), lambda b, kv, i: (b, kv, 0, 0, 0)),
        ],
        out_specs=pl.BlockSpec((1, 1, 1, group * tq, hd), lambda b, kv, i: (b, kv, i, 0, 0)),
        out_shape=jax.ShapeDtypeStruct((batch, kv_heads, nq, group * tq, hd), BF16),
        compiler_params=_cparams(("arbitrary", "arbitrary", "arbitrary"), 40),
        name="fox_prompt_attention",
    )(q_r, k_r, v_r, c, ct)
    return (o.reshape(batch, kv_heads, nq, group, tq, hd).transpose(0, 2, 4, 1, 3, 5)
            .reshape(batch * seq_len, n_heads * hd))


def _fox_sample_kernel(pt_ref, q_ref, kn_ref, vn_ref, lfr_ref, kpool, vpool, fpool, o_ref,
                       kbuf, vbuf, fbuf, sem, *, pages_per_chunk, n_chunks, page, n_tok, hd, kv_heads):
    b = pl.program_id(0)
    nb = pl.num_programs(0)
    cpp = pages_per_chunk
    rows = q_ref.shape[1]
    width = q_ref.shape[2]
    heads = rows // n_tok

    def copies(seq, chunk, slot):
        out = []
        for pi in range(cpp):
            pg = pt_ref[seq, chunk * cpp + pi]
            out.append(pltpu.make_async_copy(kpool.at[pg], kbuf.at[slot, pi], sem.at[slot]))
            out.append(pltpu.make_async_copy(vpool.at[pg], vbuf.at[slot, pi], sem.at[slot]))
            out.append(pltpu.make_async_copy(fpool.at[pg], fbuf.at[slot, pi], sem.at[slot]))
        return out

    def start(seq, chunk, slot):
        for cp in copies(seq, chunk, slot):
            cp.start()

    def wait(seq, chunk, slot):
        for cp in copies(seq, chunk, slot):
            cp.wait()

    @pl.when(b == 0)
    def _():
        start(0, n_chunks - 1, 0)

    q = q_ref[0]
    lfr = lfr_ref[0]
    tcol = lax.broadcasted_iota(jnp.int32, (rows, n_tok), 1)
    trow = lax.broadcasted_iota(jnp.int32, (rows, n_tok), 0) % n_tok
    causal = tcol <= trow
    cq = jnp.sum(jnp.where(causal, lfr, 0.0), axis=1, keepdims=True)
    ur = lax.broadcasted_iota(jnp.int32, (n_tok, n_tok), 0)
    uc = lax.broadcasted_iota(jnp.int32, (n_tok, n_tok), 1)
    triu = jnp.where(ur <= uc, 1.0, 0.0).astype(F32)
    ckn = _dot_exact(lfr, triu)

    s = _dot_nt(q, kn_ref[0].astype(BF16))
    s = jnp.where(causal, s + (cq - ckn), NEG_INF)
    m0 = jnp.max(s, axis=1, keepdims=True)
    p = jnp.exp(s - m0)
    l0 = jnp.sum(p, axis=1, keepdims=True)
    acc0 = _dot(p.astype(BF16), vn_ref[0].astype(BF16))

    jr = lax.broadcasted_iota(jnp.int32, (page, page), 0)
    sc = lax.broadcasted_iota(jnp.int32, (page, page), 1)
    ustrict = jnp.where(jr > sc, 1.0, 0.0).astype(F32)

    def chunk_body(ci, carry):
        m, l, acc, tail = carry
        chunk = n_chunks - 1 - ci
        g = b * n_chunks + ci
        slot = g % 2
        wait(b, chunk, slot)

        @pl.when(ci + 1 < n_chunks)
        def _():
            start(b, chunk - 1, 1 - slot)

        @pl.when(jnp.logical_and(ci + 1 == n_chunks, b + 1 < nb))
        def _():
            start(b + 1, n_chunks - 1, 1 - slot)

        for pi in reversed(range(cpp)):
            kp = kbuf[slot, pi].astype(BF16)
            vp = vbuf[slot, pi].astype(BF16)
            ft = fbuf[slot, pi]
            suf = _dot_exact(ft, ustrict) + tail
            tail = tail + jnp.sum(ft, axis=1, keepdims=True)
            bias = jnp.concatenate(
                [jnp.broadcast_to(suf[hh:hh + 1, :], (n_tok, page)) for hh in range(heads)], axis=0)
            s = _dot_nt(q, kp) + (cq + bias)
            m_new = jnp.maximum(m, jnp.max(s, axis=1, keepdims=True))
            alpha = jnp.exp(m - m_new)
            p = jnp.exp(s - m_new)
            l = alpha * l + jnp.sum(p, axis=1, keepdims=True)
            acc = alpha * acc + _dot(p.astype(BF16), vp)
            m = m_new
        return m, l, acc, tail

    init = (m0, l0, acc0, jnp.zeros((heads, 1), F32))
    m, l, acc, _ = lax.fori_loop(0, n_chunks, chunk_body, init)
    o = acc / l
    rpk = rows // kv_heads
    o_ref[0] = jnp.concatenate([o[kk * rpk:(kk + 1) * rpk, kk * hd:(kk + 1) * hd] for kk in range(kv_heads)],
                               axis=0)


def _fox_sample_attention(q, k_new, v_new, logf_new, k_pool, v_pool, f_pool, page_table,
                          n_seq, n_tok, n_heads, kv_heads, hd):
    group = n_heads // kv_heads
    n_pool, page = k_pool.shape[0], k_pool.shape[1]
    n_pages = page_table.shape[1]
    width = kv_heads * hd
    rows = n_heads * n_tok
    cpp = 8 if n_pages % 8 == 0 else 1
    n_chunks = n_pages // cpp
    q_t = q.reshape(n_seq, n_tok, kv_heads, group, hd).transpose(0, 2, 3, 1, 4).reshape(n_seq, kv_heads, group * n_tok, hd)
    eye = jnp.eye(kv_heads, dtype=F32)
    q_bd = (q_t[:, :, :, None, :] * eye[None, :, None, :, None]).reshape(n_seq, rows, width).astype(BF16)
    lfr = jnp.broadcast_to(logf_new.reshape(n_seq, n_tok, n_heads).transpose(0, 2, 1)[:, :, None, :],
                           (n_seq, n_heads, n_tok, n_tok)).reshape(n_seq, rows, n_tok)
    kn = k_new.reshape(n_seq, n_tok, width)
    vn = v_new.reshape(n_seq, n_tok, width)
    kp = k_pool.reshape(n_pool, page, width)
    vp = v_pool.reshape(n_pool, page, width)
    fp = f_pool.transpose(0, 2, 1)
    kern = functools.partial(_fox_sample_kernel, pages_per_chunk=cpp, n_chunks=n_chunks, page=page,
                             n_tok=n_tok, hd=hd, kv_heads=kv_heads)
    grid_spec = pltpu.PrefetchScalarGridSpec(
        num_scalar_prefetch=1,
        grid=(n_seq,),
        in_specs=[
            pl.BlockSpec((1, rows, width), lambda b, pt: (b, 0, 0)),
            pl.BlockSpec((1, n_tok, width), lambda b, pt: (b, 0, 0)),
            pl.BlockSpec((1, n_tok, width), lambda b, pt: (b, 0, 0)),
            pl.BlockSpec((1, rows, n_tok), lambda b, pt: (b, 0, 0)),
            pl.BlockSpec(memory_space=pl.ANY),
            pl.BlockSpec(memory_space=pl.ANY),
            pl.BlockSpec(memory_space=pl.ANY),
        ],
        out_specs=pl.BlockSpec((1, rows, hd), lambda b, pt: (b, 0, 0)),
        scratch_shapes=[
            pltpu.VMEM((2, cpp, page, width), F32),
            pltpu.VMEM((2, cpp, page, width), F32),
            pltpu.VMEM((2, cpp, n_heads, page), F32),
            pltpu.SemaphoreType.DMA((2,)),
        ],
    )
    o = pl.pallas_call(
        kern,
        grid_spec=grid_spec,
        out_shape=jax.ShapeDtypeStruct((n_seq, rows, hd), F32),
        compiler_params=_cparams(("arbitrary",), 40),
        name="fox_sample_attention",
    )(page_table, q_bd, kn, vn, lfr, kp, vp, fp)
    return (o.reshape(n_seq, kv_heads, group, n_tok, hd).transpose(0, 3, 1, 2, 4)
            .reshape(n_seq * n_tok, n_heads * hd))


def _swa_proj_kernel(x_ref, g_ref, w_ref, cos_ref, sin_ref, q_ref, k_ref, v_ref, *, qd, kd, hd, q_scale):
    h = _rms(x_ref[...], g_ref[...]).astype(BF16)
    proj = _dot(h, w_ref[...])
    cos = cos_ref[...]
    sin = sin_ref[...]
    half = hd // 2

    def rope(x):
        n = x.shape[1]
        reps = n // LANES
        cs = jnp.concatenate([cos] * reps, axis=1) if reps > 1 else cos
        sn = jnp.concatenate([sin] * reps, axis=1) if reps > 1 else sin
        lane = lax.broadcasted_iota(jnp.int32, x.shape, 1) % hd
        up = pltpu.roll(x, n - half, 1)
        dn = pltpu.roll(x, half, 1)
        return x * cs + jnp.where(lane < half, up, dn) * sn

    q_ref[...] = (rope(proj[:, :qd]) * q_scale).astype(q_ref.dtype)
    k_ref[...] = rope(proj[:, qd:qd + kd])
    v_ref[...] = proj[:, qd + kd:qd + 2 * kd]


def _swa_project(x, g, w, cos, sin, *, qd, kd, hd, q_dtype):
    m, d = x.shape
    n = w.shape[1]
    tm = _row_tile(m, 512)
    kern = functools.partial(_swa_proj_kernel, qd=qd, kd=kd, hd=hd, q_scale=hd ** -0.5)
    return pl.pallas_call(
        kern,
        grid=(m // tm,),
        in_specs=[
            pl.BlockSpec((tm, d), lambda i: (i, 0)),
            pl.BlockSpec((1, d), lambda i: (0, 0)),
            pl.BlockSpec((d, n), lambda i: (0, 0)),
            pl.BlockSpec((tm, LANES), lambda i: (i, 0)),
            pl.BlockSpec((tm, LANES), lambda i: (i, 0)),
        ],
        out_specs=[
            pl.BlockSpec((tm, qd), lambda i: (i, 0)),
            pl.BlockSpec((tm, kd), lambda i: (i, 0)),
            pl.BlockSpec((tm, kd), lambda i: (i, 0)),
        ],
        out_shape=[
            jax.ShapeDtypeStruct((m, qd), q_dtype),
            jax.ShapeDtypeStruct((m, kd), F32),
            jax.ShapeDtypeStruct((m, kd), F32),
        ],
        compiler_params=_cparams(("arbitrary",), 48),
        name="swa_project",
    )(x, g.reshape(1, d), w, cos, sin)


def _local_attn_kernel(q_ref, kp_ref, vp_ref, kc_ref, vc_ref, sink_ref, o_ref, *, tq, window, first_block_has_prev):
    i = pl.program_id(2)
    q = q_ref[0, 0, 0].astype(BF16)
    rows = q.shape[0]
    kp = kp_ref[0, 0].astype(BF16)
    vp = vp_ref[0, 0].astype(BF16)
    kc = kc_ref[0, 0].astype(BF16)
    vc = vc_ref[0, 0].astype(BF16)
    w = kp.shape[0]
    t_p = lax.broadcasted_iota(jnp.int32, (rows, w), 0) % tq
    j_p = lax.broadcasted_iota(jnp.int32, (rows, w), 1)
    ok_p = (w + t_p - j_p) <= window
    if not first_block_has_prev:
        ok_p = jnp.logical_and(ok_p, i > 0)
    t_c = lax.broadcasted_iota(jnp.int32, (rows, tq), 0) % tq
    j_c = lax.broadcasted_iota(jnp.int32, (rows, tq), 1)
    ok_c = j_c <= t_c
    s_p = jnp.where(ok_p, _dot_nt(q, kp), NEG_INF)
    s_c = jnp.where(ok_c, _dot_nt(q, kc), NEG_INF)
    sink = sink_ref[0, 0]
    m = jnp.maximum(jnp.maximum(jnp.max(s_p, axis=1, keepdims=True), jnp.max(s_c, axis=1, keepdims=True)), sink)
    p_p = jnp.exp(s_p - m)
    p_c = jnp.exp(s_c - m)
    l = jnp.sum(p_p, axis=1, keepdims=True) + jnp.sum(p_c, axis=1, keepdims=True) + jnp.exp(sink - m)
    acc = _dot(p_p.astype(BF16), vp) + _dot(p_c.astype(BF16), vc)
    o_ref[0, 0, 0] = (acc / l).astype(o_ref.dtype)


def _local_attention(q_r, k_prev, v_prev, k_cur, v_cur, sink_rows, *, tq, window, prev_is_cur, out_dtype):
    b, kvh, nb, rows, hd = q_r.shape
    w = tq if prev_is_cur else k_prev.shape[2]
    if prev_is_cur:
        prev_map = lambda bb, kv, i: (bb, kv, jnp.maximum(i - 1, 0), 0)
    else:
        assert nb == 1
        prev_map = lambda bb, kv, i: (bb, kv, 0, 0)
    kern = functools.partial(_local_attn_kernel, tq=tq, window=window, first_block_has_prev=not prev_is_cur)
    return pl.pallas_call(
        kern,
        grid=(b, kvh, nb),
        in_specs=[
            pl.BlockSpec((1, 1, 1, rows, hd), lambda bb, kv, i: (bb, kv, i, 0, 0)),
            pl.BlockSpec((1, 1, w, hd), prev_map),
            pl.BlockSpec((1, 1, w, hd), prev_map),
            pl.BlockSpec((1, 1, tq, hd), lambda bb, kv, i: (bb, kv, i, 0)),
            pl.BlockSpec((1, 1, tq, hd), lambda bb, kv, i: (bb, kv, i, 0)),
            pl.BlockSpec((1, 1, rows, 1), lambda bb, kv, i: (0, kv, 0, 0)),
        ],
        out_specs=pl.BlockSpec((1, 1, 1, rows, hd), lambda bb, kv, i: (bb, kv, i, 0, 0)),
        out_shape=jax.ShapeDtypeStruct((b, kvh, nb, rows, hd), out_dtype),
        compiler_params=_cparams(("arbitrary", "arbitrary", "arbitrary"), 40),
        name="local_attention",
    )(q_r, k_prev, v_prev, k_cur, v_cur, sink_rows)


def _group_rows(q, batch, nb, tq, kv_heads, group, hd):
    return (q.reshape(batch, nb, tq, kv_heads, group, hd).transpose(0, 3, 1, 4, 2, 5)
            .reshape(batch, kv_heads, nb, group * tq, hd))


def _ungroup_rows(o, batch, nb, tq, kv_heads, group, hd):
    return (o.reshape(batch, kv_heads, nb, group, tq, hd).transpose(0, 2, 4, 1, 3, 5)
            .reshape(batch * nb * tq, kv_heads * group * hd))


def _cross_attn_kernel(q_ref, k_ref, v_ref, o_ref, *, heads, hd, scale):
    outs = []
    for h in range(heads):
        q = q_ref[0, :, h * hd:(h + 1) * hd].astype(BF16)
        k = k_ref[0, :, h * hd:(h + 1) * hd].astype(BF16)
        v = v_ref[0, :, h * hd:(h + 1) * hd].astype(BF16)
        s = _dot_nt(q, k) * scale
        m = jnp.max(s, axis=1, keepdims=True)
        p = jnp.exp(s - m)
        l = jnp.sum(p, axis=1, keepdims=True)
        outs.append(_dot(p.astype(BF16), v) / l)
    o_ref[0] = jnp.concatenate(outs, axis=1).astype(o_ref.dtype)


def _cross_attention(q, mk, mv, batch, seq_len, heads, hd, out_dtype):
    n_mem = mk.shape[1]
    width = heads * hd
    tq = _row_tile(seq_len, 512)
    nq = seq_len // tq
    kern = functools.partial(_cross_attn_kernel, heads=heads, hd=hd, scale=hd ** -0.5)
    o = pl.pallas_call(
        kern,
        grid=(batch, nq),
        in_specs=[
            pl.BlockSpec((1, tq, width), lambda b, i: (b, i, 0)),
            pl.BlockSpec((1, n_mem, width), lambda b, i: (b, 0, 0)),
            pl.BlockSpec((1, n_mem, width), lambda b, i: (b, 0, 0)),
        ],
        out_specs=pl.BlockSpec((1, tq, width), lambda b, i: (b, i, 0)),
        out_shape=jax.ShapeDtypeStruct((batch, seq_len, width), out_dtype),
        compiler_params=_cparams(("arbitrary", "arbitrary"), 40),
        name="cross_attention",
    )(q.reshape(batch, seq_len, width), mk, mv)
    return o.reshape(batch * seq_len, width)


def _s5_kernel(x_ref, g_ref, bbd_ref, cbd_ref, are_ref, aim_ref, d_ref, h0re_ref, h0im_ref,
               z_ref, lre_ref, lim_ref, st_ref, sre_ref, sim_ref, *, tl, n_gc, gc_in, gc_st, cw):
    li = pl.program_id(1)

    @pl.when(li == 0)
    def _():
        sre_ref[...] = h0re_ref[...]
        sim_ref[...] = h0im_ref[...]

    d_model = x_ref.shape[2]
    rows = tl * SUBLANES
    x = x_ref[...].reshape(rows, d_model)
    h = _rms(x, g_ref[...])
    hb = h.astype(BF16)
    for k in range(n_gc):
        st_ref[:, k * 2 * gc_st:(k + 1) * 2 * gc_st] = _dot(hb[:, k * gc_in:(k + 1) * gc_in], bbd_ref[k])

    for k in range(n_gc):
        for c in range(gc_st // cw):
            re0 = k * 2 * gc_st + c * cw
            im0 = re0 + gc_st
            sc = k * gc_st + c * cw
            ar = jnp.broadcast_to(are_ref[:, sc:sc + cw], (SUBLANES, cw))
            ai = jnp.broadcast_to(aim_ref[:, sc:sc + cw], (SUBLANES, cw))

            def body(l, carry, re0=re0, im0=im0, ar=ar, ai=ai):
                sr, si = carry
                r0 = pl.multiple_of(l * SUBLANES, SUBLANES)
                br = st_ref[pl.ds(r0, SUBLANES), re0:re0 + cw]
                bi = st_ref[pl.ds(r0, SUBLANES), im0:im0 + cw]
                nr = ar * sr - ai * si + br
                ni = ar * si + ai * sr + bi
                st_ref[pl.ds(r0, SUBLANES), re0:re0 + cw] = nr
                st_ref[pl.ds(r0, SUBLANES), im0:im0 + cw] = ni
                return nr, ni

            sr, si = lax.fori_loop(0, tl, body, (sre_ref[:, sc:sc + cw], sim_ref[:, sc:sc + cw]))
            sre_ref[:, sc:sc + cw] = sr
            sim_ref[:, sc:sc + cw] = si

    ys = [_dot(st_ref[:, k * 2 * gc_st:(k + 1) * 2 * gc_st].astype(BF16), cbd_ref[k]) for k in range(n_gc)]
    y = jnp.concatenate(ys, axis=1) + d_ref[...] * h
    z = y * (0.5 * (1.0 + jnp.tanh(GELU_C * (y + 0.044715 * (y * y * y)))))
    z_ref[...] = z.reshape(tl, SUBLANES, d_model)

    @pl.when(li == pl.num_programs(1) - 1)
    def _():
        lre_ref[...] = sre_ref[...]
        lim_ref[...] = sim_ref[...]


def _s5_scan(x_t, g, bbd, cbd, a_re, a_im, d_skip, h0_re, h0_im):
    seq_len, bt, d = x_t.shape
    n_gc, gc_in, two_gc_st = bbd.shape
    gc_st = two_gc_st // 2
    ns = n_gc * gc_st
    tl = _row_tile(seq_len, 32)
    kern = functools.partial(_s5_kernel, tl=tl, n_gc=n_gc, gc_in=gc_in, gc_st=gc_st, cw=512)
    return pl.pallas_call(
        kern,
        grid=(bt // SUBLANES, seq_len // tl),
        in_specs=[
            pl.BlockSpec((tl, SUBLANES, d), lambda b, l: (l, b, 0)),
            pl.BlockSpec((1, d), lambda b, l: (0, 0)),
            pl.BlockSpec((n_gc, gc_in, 2 * gc_st), lambda b, l: (0, 0, 0)),
            pl.BlockSpec((n_gc, 2 * gc_st, gc_in), lambda b, l: (0, 0, 0)),
            pl.BlockSpec((1, ns), lambda b, l: (0, 0)),
            pl.BlockSpec((1, ns), lambda b, l: (0, 0)),
            pl.BlockSpec((1, d), lambda b, l: (0, 0)),
            pl.BlockSpec((SUBLANES, ns), lambda b, l: (b, 0)),
            pl.BlockSpec((SUBLANES, ns), lambda b, l: (b, 0)),
        ],
        out_specs=[
            pl.BlockSpec((tl, SUBLANES, d), lambda b, l: (l, b, 0)),
            pl.BlockSpec((SUBLANES, ns), lambda b, l: (b, 0)),
            pl.BlockSpec((SUBLANES, ns), lambda b, l: (b, 0)),
        ],
        out_shape=[
            jax.ShapeDtypeStruct((seq_len, bt, d), F32),
            jax.ShapeDtypeStruct((bt, ns), F32),
            jax.ShapeDtypeStruct((bt, ns), F32),
        ],
        scratch_shapes=[
            pltpu.VMEM((tl * SUBLANES, 2 * ns), F32),
            pltpu.VMEM((SUBLANES, ns), F32),
            pltpu.VMEM((SUBLANES, ns), F32),
        ],
        compiler_params=_cparams(("arbitrary", "arbitrary"), 48),
        name="s5_scan",
    )(x_t, g.reshape(1, d), bbd, cbd, a_re, a_im, d_skip.reshape(1, d), h0_re, h0_im)


def _glu_kernel(z_ref, w_ref, x_ref, o_ref):
    zw = _dot(z_ref[...].astype(BF16), w_ref[...])
    d = x_ref.shape[1]
    o_ref[...] = x_ref[...] + zw[:, :d] * jax.nn.sigmoid(zw[:, d:])


def _glu_residual(z, w, x):
    m, d = x.shape
    tm = _row_tile(m, 512)
    return pl.pallas_call(
        _glu_kernel,
        grid=(m // tm,),
        in_specs=[
            pl.BlockSpec((tm, d), lambda i: (i, 0)),
            pl.BlockSpec((d, 2 * d), lambda i: (0, 0)),
            pl.BlockSpec((tm, d), lambda i: (i, 0)),
        ],
        out_specs=pl.BlockSpec((tm, d), lambda i: (i, 0)),
        out_shape=jax.ShapeDtypeStruct((m, d), F32),
        compiler_params=_cparams(("arbitrary",), 48),
        name="glu_residual",
    )(z, w, x)


def _s5_params(a_re, a_im, b_re, b_im, c_re, c_im, log_dt, n_gc):
    groups, n_state = a_re.shape
    gsz = b_re.shape[2]
    a = lax.complex(a_re.astype(F32), a_im.astype(F32))
    dt = jnp.exp(log_dt.astype(F32))[:, None]
    a_bar = jnp.exp(a * dt)
    b_bar = ((a_bar - 1.0) / a)[..., None] * lax.complex(b_re.astype(F32), b_im.astype(F32))
    gpc = groups // n_gc
    eye = jnp.eye(gpc, dtype=F32)

    def in_blockdiag(m):
        m = m.reshape(n_gc, gpc, n_state, gsz).transpose(0, 1, 3, 2)
        return (m[:, :, :, None, :] * eye[None, :, None, :, None]).reshape(n_gc, gpc * gsz, gpc * n_state)

    def out_blockdiag(m):
        m = m.reshape(n_gc, gpc, gsz, n_state).transpose(0, 1, 3, 2)
        return (m[:, :, :, None, :] * eye[None, :, None, :, None]).reshape(n_gc, gpc * n_state, gpc * gsz)

    bbd = jnp.concatenate([in_blockdiag(b_bar.real), in_blockdiag(b_bar.imag)], axis=2).astype(BF16)
    cbd = jnp.concatenate([out_blockdiag(c_re.astype(F32)), -out_blockdiag(c_im.astype(F32))], axis=1).astype(BF16)
    return bbd, cbd, a_bar.real.reshape(1, -1), a_bar.imag.reshape(1, -1)


def kernel(x_prompt, x_sample, mem_prompt, cache_fox_k, cache_fox_v, cache_fox_logf, state_swa_k, state_swa_v, state_ssm_re, state_ssm_im, cache_mem_k, cache_mem_v, page_table, norm_g, final_g, w_ffn_gate, w_ffn_up, w_ffn_down, w_xq, w_xkv, w_xo, w_fox_qkvf, b_fox_f, w_fox_o, w_swa_qkv, swa_sink, w_swa_o, ssm_a_re, ssm_a_im, ssm_b_re, ssm_b_im, ssm_c_re, ssm_c_im, ssm_d, ssm_log_dt, w_ssm_glu):
    batch, seq, d = x_prompt.shape
    n_seq, n_tok, _ = x_sample.shape
    depth = norm_g.shape[0]
    n_mem = mem_prompt.shape[1]
    fox_kvh, fox_hd = cache_fox_k.shape[3], cache_fox_k.shape[4]
    fox_heads = cache_fox_logf.shape[3]
    fox_qd, fox_kd = fox_heads * fox_hd, fox_kvh * fox_hd
    swa_kvh, swa_hd = state_swa_k.shape[3], state_swa_k.shape[4]
    swa_heads = swa_sink.shape[1]
    swa_qd, swa_kd = swa_heads * swa_hd, swa_kvh * swa_hd
    swa_group = swa_heads // swa_kvh
    window = state_swa_k.shape[2]
    x_heads, x_hd = cache_mem_k.shape[3], cache_mem_k.shape[4]
    xd = x_heads * x_hd
    past_len = page_table.shape[1] * cache_fox_k.shape[2]
    n_groups, n_state = ssm_a_re.shape[1], ssm_a_re.shape[2]

    yp = x_prompt.reshape(batch * seq, d)
    ys = x_sample.reshape(n_seq * n_tok, d)
    mem2d = mem_prompt.reshape(batch * n_mem, d)

    wg = w_ffn_gate.astype(BF16)
    wu = w_ffn_up.astype(BF16)
    wd = w_ffn_down.astype(BF16)
    wxq = w_xq.astype(BF16)
    wxkv = w_xkv.astype(BF16)
    wxo = w_xo.astype(BF16)
    wfox = jnp.pad(w_fox_qkvf, ((0, 0), (0, 0), (0, LANES - fox_heads))).astype(BF16)
    bfox = jnp.pad(b_fox_f, ((0, 0), (0, LANES - fox_heads))).astype(F32)
    wfoxo = w_fox_o.astype(BF16)
    wswa = w_swa_qkv.astype(BF16)
    wswao = w_swa_o.astype(BF16)
    wglu = w_ssm_glu.astype(BF16)

    half = swa_hd // 2
    inv_freq = ROPE_THETA ** (-jnp.arange(half, dtype=F32) / half)

    def rope_tables(pos, reps):
        ang = pos.astype(F32)[:, None] * inv_freq[None, :]
        cos = jnp.cos(ang)
        sin = jnp.sin(ang)
        cos_t = jnp.tile(jnp.concatenate([cos, cos], axis=1), (reps, LANES // swa_hd))
        sin_t = jnp.tile(jnp.concatenate([-sin, sin], axis=1), (reps, LANES // swa_hd))
        return cos_t, sin_t

    fox_p, fox_s, swa_p, swa_s, ssm_p, ssm_s, mem_p = [], [], [], [], [], [], []
    for l in range(depth):
        kind, j = l % 3, l // 3
        yp = _ffn(yp, norm_g[l, 0], wg[l, 0], wu[l, 0], wd[l, 0])
        ys = _ffn(ys, norm_g[l, 0], wg[l, 0], wu[l, 0], wd[l, 0])
        if kind == 0:
            qp, kp, vp, lfp, cp = _fox_project(yp, norm_g[l, 1], wfox[j], bfox[j:j + 1], seq,
                                               qd=fox_qd, kd=fox_kd, nh=fox_heads, hd=fox_hd, q_dtype=BF16)
            op = _fox_prompt_attention(qp, kp, vp, cp, batch, seq, fox_heads, fox_kvh, fox_hd)
            yp = _matmul_residual(op, wfoxo[j], yp)
            qs, ks, vs, lfs, _ = _fox_project(ys, norm_g[l, 1], wfox[j], bfox[j:j + 1], n_seq * n_tok,
                                              qd=fox_qd, kd=fox_kd, nh=fox_heads, hd=fox_hd, q_dtype=F32)
            os_ = _fox_sample_attention(qs, ks, vs, lfs, cache_fox_k[j], cache_fox_v[j], cache_fox_logf[j],
                                        page_table, n_seq, n_tok, fox_heads, fox_kvh, fox_hd)
            ys = _matmul_residual(os_, wfoxo[j], ys)
            fox_p.append((kp.reshape(batch, seq, fox_kvh, fox_hd), vp.reshape(batch, seq, fox_kvh, fox_hd),
                          lfp.reshape(batch, seq, fox_heads)))
            fox_s.append((ks.reshape(n_seq, n_tok, fox_kvh, fox_hd), vs.reshape(n_seq, n_tok, fox_kvh, fox_hd),
                          lfs.reshape(n_seq, n_tok, fox_heads)))
        elif kind == 1:
            sink_rows = jnp.broadcast_to(swa_sink[j].astype(F32).reshape(1, swa_kvh, swa_group, 1, 1),
                                         (1, swa_kvh, swa_group, window, 1))
            cos_p, sin_p = rope_tables(jnp.arange(seq), batch)
            qp, kp, vp = _swa_project(yp, norm_g[l, 1], wswa[j], cos_p, sin_p,
                                      qd=swa_qd, kd=swa_kd, hd=swa_hd, q_dtype=BF16)
            nb = seq // window
            q_r = _group_rows(qp, batch, nb, window, swa_kvh, swa_group, swa_hd)
            k_r = kp.reshape(batch, seq, swa_kvh, swa_hd).transpose(0, 2, 1, 3)
            v_r = vp.reshape(batch, seq, swa_kvh, swa_hd).transpose(0, 2, 1, 3)
            o_r = _local_attention(q_r, k_r, v_r, k_r, v_r, sink_rows.reshape(1, swa_kvh, swa_group * window, 1),
                                   tq=window, window=window, prev_is_cur=True, out_dtype=BF16)
            op = _ungroup_rows(o_r, batch, nb, window, swa_kvh, swa_group, swa_hd)
            yp = _matmul_residual(op, wswao[j], yp)
            k4 = kp.reshape(batch, seq, swa_kvh, swa_hd)
            v4 = vp.reshape(batch, seq, swa_kvh, swa_hd)
            n_keep = min(window, seq)
            swa_p.append((k4[:, seq - n_keep:], v4[:, seq - n_keep:]))
            cos_s, sin_s = rope_tables(past_len + jnp.arange(n_tok), n_seq)
            qs, ks, vs = _swa_project(ys, norm_g[l, 1], wswa[j], cos_s, sin_s,
                                      qd=swa_qd, kd=swa_kd, hd=swa_hd, q_dtype=F32)
            q_r = _group_rows(qs, n_seq, 1, n_tok, swa_kvh, swa_group, swa_hd)
            kb = state_swa_k[j].transpose(0, 2, 1, 3)
            vb = state_swa_v[j].transpose(0, 2, 1, 3)
            k_r = ks.reshape(n_seq, n_tok, swa_kvh, swa_hd).transpose(0, 2, 1, 3)
            v_r = vs.reshape(n_seq, n_tok, swa_kvh, swa_hd).transpose(0, 2, 1, 3)
            sink_s = sink_rows[:, :, :, :n_tok].reshape(1, swa_kvh, swa_group * n_tok, 1)
            o_r = _local_attention(q_r, kb, vb, k_r, v_r, sink_s, tq=n_tok, window=window, prev_is_cur=False, out_dtype=F32)
            os_ = _ungroup_rows(o_r, n_seq, 1, n_tok, swa_kvh, swa_group, swa_hd)
            ys = _matmul_residual(os_, wswao[j], ys)
            kk = jnp.concatenate([state_swa_k[j], ks.reshape(n_seq, n_tok, swa_kvh, swa_hd)], axis=1)
            vv = jnp.concatenate([state_swa_v[j], vs.reshape(n_seq, n_tok, swa_kvh, swa_hd)], axis=1)
            swa_s.append((kk[:, -window:], vv[:, -window:]))
        else:
            n_gc = 4
            bbd, cbd, a_re, a_im = _s5_params(ssm_a_re[j], ssm_a_im[j], ssm_b_re[j], ssm_b_im[j],
                                              ssm_c_re[j], ssm_c_im[j], ssm_log_dt[j], n_gc)
            ns = n_groups * n_state

            def run(y2d, nbatch, nlen, h0_re, h0_im):
                x_t = y2d.reshape(nbatch, nlen, d).transpose(1, 0, 2)
                z_t, l_re, l_im = _s5_scan(x_t, norm_g[l, 1], bbd, cbd, a_re, a_im, ssm_d[j], h0_re, h0_im)
                out_t = _glu_residual(z_t.reshape(nlen * nbatch, d), wglu[j], x_t.reshape(nlen * nbatch, d))
                y_new = out_t.reshape(nlen, nbatch, d).transpose(1, 0, 2).reshape(nbatch * nlen, d)
                return y_new, l_re.reshape(nbatch, n_groups, n_state), l_im.reshape(nbatch, n_groups, n_state)

            zeros = jnp.zeros((batch, ns), F32)
            yp, rp, ip = run(yp, batch, seq, zeros, zeros)
            ys, rn, inn = run(ys, n_seq, n_tok, state_ssm_re[j].astype(F32).reshape(n_seq, ns),
                              state_ssm_im[j].astype(F32).reshape(n_seq, ns))
            ssm_p.append((rp, ip))
            ssm_s.append((rn, inn))
        kvm = _norm_matmul(mem2d, norm_g[l, 2], wxkv[l], F32)
        mk = kvm[:, :xd].reshape(batch, n_mem, xd)
        mv = kvm[:, xd:].reshape(batch, n_mem, xd)
        mem_p.append((mk.reshape(batch, n_mem, x_heads, x_hd), mv.reshape(batch, n_mem, x_heads, x_hd)))
        qx = _norm_matmul(yp, norm_g[l, 3], wxq[l], BF16)
        ox = _cross_attention(qx, mk, mv, batch, seq, x_heads, x_hd, BF16)
        yp = _matmul_residual(ox, wxo[l], yp)
        qx = _norm_matmul(ys, norm_g[l, 3], wxq[l], F32)
        ox = _cross_attention(qx, cache_mem_k[l].reshape(n_seq, n_mem, xd), cache_mem_v[l].reshape(n_seq, n_mem, xd),
                              n_seq, n_tok, x_heads, x_hd, F32)
        ys = _matmul_residual(ox, wxo[l], ys)
        yp = _ffn(yp, norm_g[l, 4], wg[l, 1], wu[l, 1], wd[l, 1])
        ys = _ffn(ys, norm_g[l, 4], wg[l, 1], wu[l, 1], wd[l, 1])

    y_prompt = _final_norm(yp, final_g).reshape(batch, seq, d)
    y_sample = _final_norm(ys, final_g).reshape(n_seq, n_tok, d)

    def stack(entries, i):
        return jnp.stack([e[i] for e in entries])

    return (y_prompt, y_sample,
            stack(fox_p, 0), stack(fox_p, 1), stack(fox_p, 2),
            stack(swa_p, 0), stack(swa_p, 1),
            stack(ssm_p, 0), stack(ssm_p, 1),
            stack(mem_p, 0), stack(mem_p, 1),
            stack(fox_s, 0), stack(fox_s, 1), stack(fox_s, 2),
            stack(swa_s, 0), stack(swa_s, 1),
            stack(ssm_s, 0), stack(ssm_s, 1))
```

```python
import functools
import math

import jax
import jax.numpy as jnp
from jax import lax
from jax.experimental import pallas as pl
from jax.experimental.pallas import tpu as pltpu

F32 = jnp.float32
BF16 = jnp.bfloat16

RMS_EPS = 1e-6
NEG_INF = -1e30
ROPE_THETA = 10000.0
SSM_GROUP = 16
GELU_C = math.sqrt(2.0 / math.pi)

V7X_VMEM_BYTES = 64 * 1024 * 1024
LANES = 128
SUBLANES = 8


def _cparams(sem, vmem_mb):
    assert vmem_mb * 2**20 < V7X_VMEM_BYTES
    return pltpu.CompilerParams(dimension_semantics=sem, vmem_limit_bytes=vmem_mb * 2**20)


def _rms(x, g):
    ms = jnp.mean(x * x, axis=-1, keepdims=True)
    return (x * lax.rsqrt(ms + RMS_EPS)) * g


def _dot(a, b):
    return jnp.dot(a, b, preferred_element_type=F32)


def _dot_nt(a, b):
    return lax.dot_general(a, b, (((1,), (1,)), ((), ())), preferred_element_type=F32)


def _dot_exact(a, b):
    return jnp.dot(a, b, preferred_element_type=F32, precision=lax.Precision.HIGHEST)


def _row_tile(m, want):
    t = min(want, m)
    assert m % t == 0, (m, t)
    return t


def _ffn_kernel(x_ref, g_ref, wg_ref, wu_ref, wd_ref, o_ref, h_ref, acc_ref):
    j = pl.program_id(1)

    @pl.when(j == 0)
    def _():
        h_ref[...] = _rms(x_ref[...], g_ref[...]).astype(BF16)
        acc_ref[...] = jnp.zeros_like(acc_ref)

    h = h_ref[...]
    gate = _dot(h, wg_ref[...])
    up = _dot(h, wu_ref[...])
    act = (gate * jax.nn.sigmoid(gate) * up).astype(BF16)
    acc_ref[...] += _dot(act, wd_ref[...])

    @pl.when(j == pl.num_programs(1) - 1)
    def _():
        o_ref[...] = x_ref[...] + 0.5 * acc_ref[...]


def _ffn(x, g, wg, wu, wd):
    m, d = x.shape
    ff = wg.shape[1]
    tm = _row_tile(m, 512)
    tf = ff // 2 if (ff // 2) % LANES == 0 else 256
    assert ff % tf == 0
    return pl.pallas_call(
        _ffn_kernel,
        grid=(m // tm, ff // tf),
        in_specs=[
            pl.BlockSpec((tm, d), lambda i, j: (i, 0)),
            pl.BlockSpec((1, d), lambda i, j: (0, 0)),
            pl.BlockSpec((d, tf), lambda i, j: (0, j)),
            pl.BlockSpec((d, tf), lambda i, j: (0, j)),
            pl.BlockSpec((tf, d), lambda i, j: (j, 0)),
        ],
        out_specs=pl.BlockSpec((tm, d), lambda i, j: (i, 0)),
        out_shape=jax.ShapeDtypeStruct((m, d), F32),
        scratch_shapes=[pltpu.VMEM((tm, d), BF16), pltpu.VMEM((tm, d), F32)],
        compiler_params=_cparams(("arbitrary", "arbitrary"), 48),
        name="ffn",
    )(x, g.reshape(1, d), wg, wu, wd)


def _nmm_kernel(x_ref, g_ref, w_ref, o_ref):
    h = _rms(x_ref[...], g_ref[...]).astype(BF16)
    o_ref[...] = _dot(h, w_ref[...]).astype(o_ref.dtype)


def _norm_matmul(x, g, w, out_dtype):
    m, d = x.shape
    n = w.shape[1]
    tm = _row_tile(m, 512)
    return pl.pallas_call(
        _nmm_kernel,
        grid=(m // tm,),
        in_specs=[
            pl.BlockSpec((tm, d), lambda i: (i, 0)),
            pl.BlockSpec((1, d), lambda i: (0, 0)),
            pl.BlockSpec((d, n), lambda i: (0, 0)),
        ],
        out_specs=pl.BlockSpec((tm, n), lambda i: (i, 0)),
        out_shape=jax.ShapeDtypeStruct((m, n), out_dtype),
        compiler_params=_cparams(("arbitrary",), 40),
        name="norm_matmul",
    )(x, g.reshape(1, d), w)


def _mmres_kernel(a_ref, w_ref, x_ref, o_ref):
    o_ref[...] = x_ref[...] + _dot(a_ref[...].astype(BF16), w_ref[...])


def _matmul_residual(a, w, x):
    m, k = a.shape
    d = w.shape[1]
    tm = _row_tile(m, 512)
    return pl.pallas_call(
        _mmres_kernel,
        grid=(m // tm,),
        in_specs=[
            pl.BlockSpec((tm, k), lambda i: (i, 0)),
            pl.BlockSpec((k, d), lambda i: (0, 0)),
            pl.BlockSpec((tm, d), lambda i: (i, 0)),
        ],
        out_specs=pl.BlockSpec((tm, d), lambda i: (i, 0)),
        out_shape=jax.ShapeDtypeStruct((m, d), F32),
        compiler_params=_cparams(("arbitrary",), 40),
        name="matmul_residual",
    )(a, w, x)


def _final_norm_kernel(x_ref, g_ref, o_ref):
    o_ref[...] = _rms(x_ref[...], g_ref[...])


def _final_norm(x, g):
    m, d = x.shape
    tm = _row_tile(m, 1024)
    return pl.pallas_call(
        _final_norm_kernel,
        grid=(m // tm,),
        in_specs=[pl.BlockSpec((tm, d), lambda i: (i, 0)), pl.BlockSpec((1, d), lambda i: (0, 0))],
        out_specs=pl.BlockSpec((tm, d), lambda i: (i, 0)),
        out_shape=jax.ShapeDtypeStruct((m, d), F32),
        compiler_params=_cparams(("arbitrary",), 40),
        name="final_norm",
    )(x, g.reshape(1, d))


def _fox_proj_kernel(x_ref, g_ref, w_ref, b_ref, q_ref, k_ref, v_ref, lf_ref, c_ref, carry_ref,
                     *, qd, kd, nh, tiles_per_seq, q_scale):
    i = pl.program_id(0)
    h = _rms(x_ref[...], g_ref[...]).astype(BF16)
    proj = _dot(h, w_ref[...])
    q_ref[...] = (proj[:, :qd] * q_scale).astype(q_ref.dtype)
    k_ref[...] = proj[:, qd:qd + kd]
    v_ref[...] = proj[:, qd + kd:qd + 2 * kd]
    z = proj[:, qd + 2 * kd:] + b_ref[...]
    lf = jnp.minimum(z, 0.0) - jnp.log1p(jnp.exp(-jnp.abs(z)))
    lf_ref[...] = lf[:, :nh]

    @pl.when(i % tiles_per_seq == 0)
    def _():
        carry_ref[...] = jnp.zeros_like(carry_ref)

    tm = lf.shape[0]
    r = lax.broadcasted_iota(jnp.int32, (tm, tm), 0)
    c = lax.broadcasted_iota(jnp.int32, (tm, tm), 1)
    tril = jnp.where(c <= r, 1.0, 0.0).astype(F32)
    csum = _dot_exact(tril, lf) + carry_ref[...]
    c_ref[...] = csum[:, :nh]
    carry_ref[...] = csum[tm - 1:tm, :]


def _fox_project(x, g, w_pad, b_pad, seq_len, *, qd, kd, nh, hd, q_dtype):
    m, d = x.shape
    n = w_pad.shape[1]
    tm = _row_tile(seq_len, 512)
    kern = functools.partial(_fox_proj_kernel, qd=qd, kd=kd, nh=nh, tiles_per_seq=seq_len // tm,
                             q_scale=hd ** -0.5)
    return pl.pallas_call(
        kern,
        grid=(m // tm,),
        in_specs=[
            pl.BlockSpec((tm, d), lambda i: (i, 0)),
            pl.BlockSpec((1, d), lambda i: (0, 0)),
            pl.BlockSpec((d, n), lambda i: (0, 0)),
            pl.BlockSpec((1, LANES), lambda i: (0, 0)),
        ],
        out_specs=[
            pl.BlockSpec((tm, qd), lambda i: (i, 0)),
            pl.BlockSpec((tm, kd), lambda i: (i, 0)),
            pl.BlockSpec((tm, kd), lambda i: (i, 0)),
            pl.BlockSpec((tm, nh), lambda i: (i, 0)),
            pl.BlockSpec((tm, nh), lambda i: (i, 0)),
        ],
        out_shape=[
            jax.ShapeDtypeStruct((m, qd), q_dtype),
            jax.ShapeDtypeStruct((m, kd), F32),
            jax.ShapeDtypeStruct((m, kd), F32),
            jax.ShapeDtypeStruct((m, nh), F32),
            jax.ShapeDtypeStruct((m, nh), F32),
        ],
        scratch_shapes=[pltpu.VMEM((1, LANES), F32)],
        compiler_params=_cparams(("arbitrary",), 48),
        name="fox_project",
    )(x, g.reshape(1, d), w_pad, b_pad)


def _split3(x):
    a = lax.reduce_precision(x, exponent_bits=8, mantissa_bits=7)
    r = x - a
    b = lax.reduce_precision(r, exponent_bits=8, mantissa_bits=7)
    return a.astype(BF16), b.astype(BF16), (r - b).astype(BF16)


def _fox_prompt_kernel(q_ref, k_ref, v_ref, o_ref, *, tq):
    i = pl.program_id(2)
    q = q_ref[0, 0, 0]
    rows = q.shape[0]

    def step(j, carry, masked):
        m, l, acc = carry
        start = pl.multiple_of(j * tq, tq)
        k_j = k_ref[0, 0, pl.ds(start, tq), :]
        v_j = v_ref[0, 0, pl.ds(start, tq), :]
        s = _dot_nt(q, k_j)
        if masked:
            r = lax.broadcasted_iota(jnp.int32, (rows, tq), 0) % tq
            cc = lax.broadcasted_iota(jnp.int32, (rows, tq), 1)
            s = jnp.where(cc <= r, s, NEG_INF)
        m_new = jnp.maximum(m, jnp.max(s, axis=1, keepdims=True))
        alpha = jnp.exp(m - m_new)
        p = jnp.exp(s - m_new)
        l = alpha * l + jnp.sum(p, axis=1, keepdims=True)
        acc = alpha * acc + _dot(p.astype(BF16), v_j)
        return m_new, l, acc

    hd = v_ref.shape[3]
    init = (jnp.full((rows, 1), NEG_INF, F32), jnp.zeros((rows, 1), F32), jnp.zeros((rows, hd), F32))
    carry = lax.fori_loop(0, i, lambda j, cr: step(j, cr, False), init)
    m, l, acc = step(i, carry, True)
    o_ref[0, 0, 0] = (acc / l).astype(o_ref.dtype)


def _fox_prompt_attention(q, k, v, c, batch, seq_len, n_heads, kv_heads, hd):
    group = n_heads // kv_heads
    tq = _row_tile(seq_len, 256)
    nq = seq_len // tq
    nterm = 3
    assert hd + 2 * nterm * group <= LANES
    c1, c2, c3 = _split3(c.reshape(batch, seq_len, kv_heads, group))
    one = jnp.ones_like(c1)
    eye = jnp.eye(group, dtype=BF16)
    q_bias = jnp.stack([c1, c2, c3, one, one, one], axis=-1)
    q_bias = (q_bias[:, :, :, :, None, :] * eye[None, None, None, :, :, None]).reshape(
        batch, seq_len, kv_heads, group, 2 * nterm * group)
    k_bias = jnp.stack([one, one, one, -c1, -c2, -c3], axis=-1).reshape(
        batch, seq_len, kv_heads, 2 * nterm * group)
    pad = LANES - hd - 2 * nterm * group
    q_aug = jnp.concatenate([q.reshape(batch, seq_len, kv_heads, group, hd), q_bias,
                             jnp.zeros((batch, seq_len, kv_heads, group, pad), BF16)], axis=-1)
    k_aug = jnp.concatenate([k.astype(BF16).reshape(batch, seq_len, kv_heads, hd), k_bias,
                             jnp.zeros((batch, seq_len, kv_heads, pad), BF16)], axis=-1)
    q_r = (q_aug.reshape(batch, nq, tq, kv_heads, group, LANES).transpose(0, 3, 1, 4, 2, 5)
           .reshape(batch, kv_heads, nq, group * tq, LANES))
    k_r = k_aug.transpose(0, 2, 1, 3)
    v_r = v.astype(BF16).reshape(batch, seq_len, kv_heads, hd).transpose(0, 2, 1, 3)
    kern = functools.partial(_fox_prompt_kernel, tq=tq)
    o = pl.pallas_call(
        kern,
        grid=(batch, kv_heads, nq),
        in_specs=[
            pl.BlockSpec((1, 1, 1, group * tq, LANES), lambda b, kv, i: (b, kv, i, 0, 0)),
            pl.BlockSpec((1, 1, seq_len, LANES), lambda b, kv, i: (b, kv, 0, 0)),
            pl.BlockSpec((1, 1, seq_len, hd), lambda b, kv, i: (b, kv, 0, 0)),
        ],
        out_specs=pl.BlockSpec((1, 1, 1, group * tq, hd), lambda b, kv, i: (b, kv, i, 0, 0)),
        out_shape=jax.ShapeDtypeStruct((batch, kv_heads, nq, group * tq, hd), BF16),
        compiler_params=_cparams(("arbitrary", "arbitrary", "arbitrary"), 40),
        name="fox_prompt_attention",
    )(q_r, k_r, v_r)
    return (o.reshape(batch, kv_heads, nq, group, tq, hd).transpose(0, 2, 4, 1, 3, 5)
            .reshape(batch * seq_len, n_heads * hd))


def _row_to_col(x):
    n = x.shape[1]
    return jnp.transpose(jnp.broadcast_to(x, (n, n)))[:, :1]


def _fox_sample_kernel(pt_ref, qbd_ref, kn_ref, vn_ref, lft_ref, kpool, vpool, fpool, o_ref,
                       kbuf, vbuf, fbuf, sem, *, layer, cpp, n_chunks, page, n_tok, heads, hd, kv_heads):
    b = pl.program_id(0)
    nb = pl.num_programs(0)
    cols = n_tok * heads
    group = heads // kv_heads

    def copies(seq, chunk, slot):
        out = []
        for pi in range(cpp):
            pg = pt_ref[seq, chunk * cpp + pi]
            rows_pi = pl.ds(pi * page, page)
            out.append(pltpu.make_async_copy(kpool.at[layer, pg], kbuf.at[slot, rows_pi], sem.at[slot]))
            out.append(pltpu.make_async_copy(vpool.at[layer, pg], vbuf.at[slot, rows_pi], sem.at[slot]))
            out.append(pltpu.make_async_copy(fpool.at[layer, pg], fbuf.at[slot, pi], sem.at[slot]))
        return out

    def start(seq, chunk, slot):
        for cp in copies(seq, chunk, slot):
            cp.start()

    def wait(seq, chunk, slot):
        for cp in copies(seq, chunk, slot):
            cp.wait()

    @pl.when(b == 0)
    def _():
        start(0, n_chunks - 1, 0)

    def dot_keys(p, v):
        return lax.dot_general(p, v, (((0,), (0,)), ((), ())), preferred_element_type=F32)

    qbd = qbd_ref[0]
    lft = lft_ref[0]
    krow = lax.broadcasted_iota(jnp.int32, (page, cols), 0)
    tcol = lax.broadcasted_iota(jnp.int32, (page, cols), 1) // heads
    pr = lax.broadcasted_iota(jnp.int32, (page, page), 0)
    pc = lax.broadcasted_iota(jnp.int32, (page, page), 1)
    tril = jnp.where(pc <= pr, 1.0, 0.0).astype(F32)
    cnew = _dot_exact(tril, lft)
    cq = jnp.sum(jnp.where(krow == tcol, cnew, 0.0), axis=0, keepdims=True)

    s = _dot(kn_ref[0].astype(BF16), qbd)
    s = jnp.where(krow <= tcol, s + (cq - cnew), NEG_INF)
    m0 = jnp.max(s, axis=0, keepdims=True)
    p = jnp.exp(s - m0)
    l0 = jnp.sum(p, axis=0, keepdims=True)
    acc0 = dot_keys(p.astype(BF16), vn_ref[0].astype(BF16))

    lane = lax.broadcasted_iota(jnp.int32, (1, cols), 1)
    seg = lane // heads

    def chunk_body(ci, carry):
        m, l, acc, tail = carry
        chunk = n_chunks - 1 - ci
        g = b * n_chunks + ci
        slot = g % 2
        wait(b, chunk, slot)

        @pl.when(ci + 1 < n_chunks)
        def _():
            start(b, chunk - 1, 1 - slot)

        @pl.when(jnp.logical_and(ci + 1 == n_chunks, b + 1 < nb))
        def _():
            start(b + 1, n_chunks - 1, 1 - slot)

        f = jnp.concatenate([fbuf[slot, pi] for pi in range(cpp)], axis=1)
        r = f
        k = 1
        while k < page:
            r = r + jnp.where(krow < page - k, pltpu.roll(r, page - k, 0), 0.0)
            k *= 2
        tot = r[0:1, :]
        inc = tot
        full = tot
        k = 1
        while k < cpp:
            inc = inc + jnp.where(lane < cols - k * heads, pltpu.roll(inc, cols - k * heads, 1), 0.0)
            full = full + pltpu.roll(full, k * heads, 1)
            k *= 2
        suf = (r - f) + ((inc - tot) + tail)

        kc = kbuf[slot].astype(BF16)
        vc = vbuf[slot].astype(BF16)
        s_all = _dot(kc, qbd)
        parts = []
        for pi in range(cpp):
            y = jnp.where(seg == pi, suf, 0.0)
            k = 1
            while k < n_tok:
                y = y + pltpu.roll(y, k * heads, 1)
                k *= 2
            parts.append(s_all[pi * page:(pi + 1) * page] + (y + cq))
        s = jnp.concatenate(parts, axis=0)
        m_new = jnp.maximum(m, jnp.max(s, axis=0, keepdims=True))
        alpha = jnp.exp(m - m_new)
        p = jnp.exp(s - m_new)
        l = alpha * l + jnp.sum(p, axis=0, keepdims=True)
        acc = _row_to_col(alpha) * acc + dot_keys(p.astype(BF16), vc)
        return m_new, l, acc, tail + full

    init = (m0, l0, acc0, jnp.zeros((1, cols), F32))
    m, l, acc, _ = lax.fori_loop(0, n_chunks, chunk_body, init)
    o = acc / _row_to_col(l)
    row_kv = (lax.broadcasted_iota(jnp.int32, (cols, 1), 0) % heads) // group
    out = o[:, :hd]
    for kk in range(1, kv_heads):
        out = jnp.where(row_kv == kk, o[:, kk * hd:(kk + 1) * hd], out)
    o_ref[0] = out


def _fox_sample_attention(q, k_new, v_new, logf_new, k_pools, v_pools, f_pools, layer, page_table,
                          n_seq, n_tok, n_heads, kv_heads, hd):
    group = n_heads // kv_heads
    n_layers, n_pool, page = k_pools.shape[0], k_pools.shape[1], k_pools.shape[2]
    n_pages = page_table.shape[1]
    width = kv_heads * hd
    cols = n_tok * n_heads
    cpp = n_tok
    assert cols == LANES and n_pages % cpp == 0 and cpp & (cpp - 1) == 0 and page & (page - 1) == 0
    n_chunks = n_pages // cpp
    head_kv = jnp.arange(n_heads) // group
    kv_mask = (head_kv[None, :] == jnp.arange(kv_heads)[:, None]).astype(F32)
    q4 = q.reshape(n_seq, n_tok, n_heads, hd).transpose(0, 3, 1, 2)
    qbd = (q4[:, None] * kv_mask[None, :, None, None, :]).reshape(n_seq, width, cols).astype(BF16)
    pad = ((0, 0), (0, page - n_tok), (0, 0))
    lft = jnp.pad(jnp.tile(logf_new.reshape(n_seq, n_tok, n_heads), (1, 1, n_tok)), pad)
    kn = jnp.pad(k_new.reshape(n_seq, n_tok, width), pad)
    vn = jnp.pad(v_new.reshape(n_seq, n_tok, width), pad)
    kp = k_pools.reshape(n_layers, n_pool, page, width)
    vp = v_pools.reshape(n_layers, n_pool, page, width)
    kern = functools.partial(_fox_sample_kernel, layer=layer, cpp=cpp, n_chunks=n_chunks, page=page,
                             n_tok=n_tok, heads=n_heads, hd=hd, kv_heads=kv_heads)
    grid_spec = pltpu.PrefetchScalarGridSpec(
        num_scalar_prefetch=1,
        grid=(n_seq,),
        in_specs=[
            pl.BlockSpec((1, width, cols), lambda b, pt: (b, 0, 0)),
            pl.BlockSpec((1, page, width), lambda b, pt: (b, 0, 0)),
            pl.BlockSpec((1, page, width), lambda b, pt: (b, 0, 0)),
            pl.BlockSpec((1, page, cols), lambda b, pt: (b, 0, 0)),
            pl.BlockSpec(memory_space=pl.ANY),
            pl.BlockSpec(memory_space=pl.ANY),
            pl.BlockSpec(memory_space=pl.ANY),
        ],
        out_specs=pl.BlockSpec((1, cols, hd), lambda b, pt: (b, 0, 0)),
        scratch_shapes=[
            pltpu.VMEM((2, cpp * page, width), F32),
            pltpu.VMEM((2, cpp * page, width), F32),
            pltpu.VMEM((2, cpp, page, n_heads), F32),
            pltpu.SemaphoreType.DMA((2,)),
        ],
    )
    o = pl.pallas_call(
        kern,
        grid_spec=grid_spec,
        out_shape=jax.ShapeDtypeStruct((n_seq, cols, hd), F32),
        compiler_params=_cparams(("arbitrary",), 40),
        name="fox_sample_attention",
    )(page_table, qbd, kn, vn, lft, kp, vp, f_pools)
    return o.reshape(n_seq * n_tok, n_heads * hd)


def _swa_proj_kernel(x_ref, g_ref, w_ref, cos_ref, sin_ref, q_ref, k_ref, v_ref, *, qd, kd, hd, q_scale):
    h = _rms(x_ref[...], g_ref[...]).astype(BF16)
    proj = _dot(h, w_ref[...])
    cos = cos_ref[...]
    sin = sin_ref[...]
    half = hd // 2

    def rope(x):
        n = x.shape[1]
        reps = n // LANES
        cs = jnp.concatenate([cos] * reps, axis=1) if reps > 1 else cos
        sn = jnp.concatenate([sin] * reps, axis=1) if reps > 1 else sin
        lane = lax.broadcasted_iota(jnp.int32, x.shape, 1) % hd
        up = pltpu.roll(x, n - half, 1)
        dn = pltpu.roll(x, half, 1)
        return x * cs + jnp.where(lane < half, up, dn) * sn

    q_ref[...] = (rope(proj[:, :qd]) * q_scale).astype(q_ref.dtype)
    k_ref[...] = rope(proj[:, qd:qd + kd])
    v_ref[...] = proj[:, qd + kd:qd + 2 * kd]


def _swa_project(x, g, w, cos, sin, *, qd, kd, hd, q_dtype):
    m, d = x.shape
    n = w.shape[1]
    tm = _row_tile(m, 512)
    kern = functools.partial(_swa_proj_kernel, qd=qd, kd=kd, hd=hd, q_scale=hd ** -0.5)
    return pl.pallas_call(
        kern,
        grid=(m // tm,),
        in_specs=[
            pl.BlockSpec((tm, d), lambda i: (i, 0)),
            pl.BlockSpec((1, d), lambda i: (0, 0)),
            pl.BlockSpec((d, n), lambda i: (0, 0)),
            pl.BlockSpec((tm, LANES), lambda i: (i, 0)),
            pl.BlockSpec((tm, LANES), lambda i: (i, 0)),
        ],
        out_specs=[
            pl.BlockSpec((tm, qd), lambda i: (i, 0)),
            pl.BlockSpec((tm, kd), lambda i: (i, 0)),
            pl.BlockSpec((tm, kd), lambda i: (i, 0)),
        ],
        out_shape=[
            jax.ShapeDtypeStruct((m, qd), q_dtype),
            jax.ShapeDtypeStruct((m, kd), F32),
            jax.ShapeDtypeStruct((m, kd), F32),
        ],
        compiler_params=_cparams(("arbitrary",), 48),
        name="swa_project",
    )(x, g.reshape(1, d), w, cos, sin)


def _local_attn_kernel(q_ref, kp_ref, vp_ref, kc_ref, vc_ref, sink_ref, o_ref, *, tq, window, first_block_has_prev):
    i = pl.program_id(2)
    q = q_ref[0, 0, 0].astype(BF16)
    rows = q.shape[0]
    kp = kp_ref[0, 0].astype(BF16)
    vp = vp_ref[0, 0].astype(BF16)
    kc = kc_ref[0, 0].astype(BF16)
    vc = vc_ref[0, 0].astype(BF16)
    w = kp.shape[0]
    t_p = lax.broadcasted_iota(jnp.int32, (rows, w), 0) % tq
    j_p = lax.broadcasted_iota(jnp.int32, (rows, w), 1)
    ok_p = (w + t_p - j_p) <= window
    if not first_block_has_prev:
        ok_p = jnp.logical_and(ok_p, i > 0)
    t_c = lax.broadcasted_iota(jnp.int32, (rows, tq), 0) % tq
    j_c = lax.broadcasted_iota(jnp.int32, (rows, tq), 1)
    ok_c = j_c <= t_c
    s_p = jnp.where(ok_p, _dot_nt(q, kp), NEG_INF)
    s_c = jnp.where(ok_c, _dot_nt(q, kc), NEG_INF)
    sink = sink_ref[0, 0]
    m = jnp.maximum(jnp.maximum(jnp.max(s_p, axis=1, keepdims=True), jnp.max(s_c, axis=1, keepdims=True)), sink)
    p_p = jnp.exp(s_p - m)
    p_c = jnp.exp(s_c - m)
    l = jnp.sum(p_p, axis=1, keepdims=True) + jnp.sum(p_c, axis=1, keepdims=True) + jnp.exp(sink - m)
    acc = _dot(p_p.astype(BF16), vp) + _dot(p_c.astype(BF16), vc)
    o_ref[0, 0, 0] = (acc / l).astype(o_ref.dtype)


def _local_attention(q_r, k_prev, v_prev, k_cur, v_cur, sink_rows, *, tq, window, prev_is_cur, out_dtype):
    b, kvh, nb, rows, hd = q_r.shape
    w = tq if prev_is_cur else k_prev.shape[2]
    if prev_is_cur:
        prev_map = lambda bb, kv, i: (bb, kv, jnp.maximum(i - 1, 0), 0)
    else:
        assert nb == 1
        prev_map = lambda bb, kv, i: (bb, kv, 0, 0)
    kern = functools.partial(_local_attn_kernel, tq=tq, window=window, first_block_has_prev=not prev_is_cur)
    return pl.pallas_call(
        kern,
        grid=(b, kvh, nb),
        in_specs=[
            pl.BlockSpec((1, 1, 1, rows, hd), lambda bb, kv, i: (bb, kv, i, 0, 0)),
            pl.BlockSpec((1, 1, w, hd), prev_map),
            pl.BlockSpec((1, 1, w, hd), prev_map),
            pl.BlockSpec((1, 1, tq, hd), lambda bb, kv, i: (bb, kv, i, 0)),
            pl.BlockSpec((1, 1, tq, hd), lambda bb, kv, i: (bb, kv, i, 0)),
            pl.BlockSpec((1, 1, rows, 1), lambda bb, kv, i: (0, kv, 0, 0)),
        ],
        out_specs=pl.BlockSpec((1, 1, 1, rows, hd), lambda bb, kv, i: (bb, kv, i, 0, 0)),
        out_shape=jax.ShapeDtypeStruct((b, kvh, nb, rows, hd), out_dtype),
        compiler_params=_cparams(("arbitrary", "arbitrary", "arbitrary"), 40),
        name="local_attention",
    )(q_r, k_prev, v_prev, k_cur, v_cur, sink_rows)


def _group_rows(q, batch, nb, tq, kv_heads, group, hd):
    return (q.reshape(batch, nb, tq, kv_heads, group, hd).transpose(0, 3, 1, 4, 2, 5)
            .reshape(batch, kv_heads, nb, group * tq, hd))


def _ungroup_rows(o, batch, nb, tq, kv_heads, group, hd):
    return (o.reshape(batch, kv_heads, nb, group, tq, hd).transpose(0, 2, 4, 1, 3, 5)
            .reshape(batch * nb * tq, kv_heads * group * hd))


def _cross_attn_kernel(q_ref, k_ref, v_ref, o_ref, *, heads, hd, scale):
    outs = []
    for h in range(heads):
        q = q_ref[0, :, h * hd:(h + 1) * hd].astype(BF16)
        k = k_ref[0, 0, :, h * hd:(h + 1) * hd].astype(BF16)
        v = v_ref[0, 0, :, h * hd:(h + 1) * hd].astype(BF16)
        s = _dot_nt(q, k) * scale
        m = jnp.max(s, axis=1, keepdims=True)
        p = jnp.exp(s - m)
        l = jnp.sum(p, axis=1, keepdims=True)
        outs.append(_dot(p.astype(BF16), v) / l)
    o_ref[0] = jnp.concatenate(outs, axis=1).astype(o_ref.dtype)


def _cross_attention(q, mk, mv, k_col, v_col, layer, batch, seq_len, heads, hd, out_dtype):
    n_mem = mk.shape[2]
    width = heads * hd
    tq = _row_tile(seq_len, 512)
    nq = seq_len // tq
    kern = functools.partial(_cross_attn_kernel, heads=heads, hd=hd, scale=hd ** -0.5)
    o = pl.pallas_call(
        kern,
        grid=(batch, nq),
        in_specs=[
            pl.BlockSpec((1, tq, width), lambda b, i: (b, i, 0)),
            pl.BlockSpec((1, 1, n_mem, width), lambda b, i: (layer, b, 0, k_col)),
            pl.BlockSpec((1, 1, n_mem, width), lambda b, i: (layer, b, 0, v_col)),
        ],
        out_specs=pl.BlockSpec((1, tq, width), lambda b, i: (b, i, 0)),
        out_shape=jax.ShapeDtypeStruct((batch, seq_len, width), out_dtype),
        compiler_params=_cparams(("arbitrary", "arbitrary"), 40),
        name="cross_attention",
    )(q.reshape(batch, seq_len, width), mk, mv)
    return o.reshape(batch * seq_len, width)


def _s5_kernel(x_ref, g_ref, bbd_ref, cbd_ref, are_ref, aim_ref, d_ref, h0re_ref, h0im_ref,
               z_ref, lre_ref, lim_ref, st_ref, sre_ref, sim_ref, *, tl, n_gc, gc_in, gc_st, cw):
    li = pl.program_id(1)

    @pl.when(li == 0)
    def _():
        sre_ref[...] = h0re_ref[...]
        sim_ref[...] = h0im_ref[...]

    d_model = x_ref.shape[2]
    rows = tl * SUBLANES
    x = x_ref[...].reshape(rows, d_model)
    h = _rms(x, g_ref[...])
    hb = h.astype(BF16)
    for k in range(n_gc):
        st_ref[:, k * 2 * gc_st:(k + 1) * 2 * gc_st] = _dot(hb[:, k * gc_in:(k + 1) * gc_in], bbd_ref[k])

    for k in range(n_gc):
        for c in range(gc_st // cw):
            re0 = k * 2 * gc_st + c * cw
            im0 = re0 + gc_st
            sc = k * gc_st + c * cw
            ar = jnp.broadcast_to(are_ref[:, sc:sc + cw], (SUBLANES, cw))
            ai = jnp.broadcast_to(aim_ref[:, sc:sc + cw], (SUBLANES, cw))

            def body(l, carry, re0=re0, im0=im0, ar=ar, ai=ai):
                sr, si = carry
                r0 = pl.multiple_of(l * SUBLANES, SUBLANES)
                br = st_ref[pl.ds(r0, SUBLANES), re0:re0 + cw]
                bi = st_ref[pl.ds(r0, SUBLANES), im0:im0 + cw]
                nr = ar * sr - ai * si + br
                ni = ar * si + ai * sr + bi
                st_ref[pl.ds(r0, SUBLANES), re0:re0 + cw] = nr
                st_ref[pl.ds(r0, SUBLANES), im0:im0 + cw] = ni
                return nr, ni

            sr, si = lax.fori_loop(0, tl, body, (sre_ref[:, sc:sc + cw], sim_ref[:, sc:sc + cw]))
            sre_ref[:, sc:sc + cw] = sr
            sim_ref[:, sc:sc + cw] = si

    ys = [_dot(st_ref[:, k * 2 * gc_st:(k + 1) * 2 * gc_st].astype(BF16), cbd_ref[k]) for k in range(n_gc)]
    y = jnp.concatenate(ys, axis=1) + d_ref[...] * h
    z = y * (0.5 * (1.0 + jnp.tanh(GELU_C * (y + 0.044715 * (y * y * y)))))
    z_ref[...] = z.reshape(tl, SUBLANES, d_model)

    @pl.when(li == pl.num_programs(1) - 1)
    def _():
        lre_ref[...] = sre_ref[...]
        lim_ref[...] = sim_ref[...]


def _s5_scan(x_t, g, bbd, cbd, a_re, a_im, d_skip, h0_re, h0_im):
    seq_len, bt, d = x_t.shape
    n_gc, gc_in, two_gc_st = bbd.shape
    gc_st = two_gc_st // 2
    ns = n_gc * gc_st
    tl = _row_tile(seq_len, 32)
    kern = functools.partial(_s5_kernel, tl=tl, n_gc=n_gc, gc_in=gc_in, gc_st=gc_st, cw=512)
    return pl.pallas_call(
        kern,
        grid=(bt // SUBLANES, seq_len // tl),
        in_specs=[
            pl.BlockSpec((tl, SUBLANES, d), lambda b, l: (l, b, 0)),
            pl.BlockSpec((1, d), lambda b, l: (0, 0)),
            pl.BlockSpec((n_gc, gc_in, 2 * gc_st), lambda b, l: (0, 0, 0)),
            pl.BlockSpec((n_gc, 2 * gc_st, gc_in), lambda b, l: (0, 0, 0)),
            pl.BlockSpec((1, ns), lambda b, l: (0, 0)),
            pl.BlockSpec((1, ns), lambda b, l: (0, 0)),
            pl.BlockSpec((1, d), lambda b, l: (0, 0)),
            pl.BlockSpec((SUBLANES, ns), lambda b, l: (b, 0)),
            pl.BlockSpec((SUBLANES, ns), lambda b, l: (b, 0)),
        ],
        out_specs=[
            pl.BlockSpec((tl, SUBLANES, d), lambda b, l: (l, b, 0)),
            pl.BlockSpec((SUBLANES, ns), lambda b, l: (b, 0)),
            pl.BlockSpec((SUBLANES, ns), lambda b, l: (b, 0)),
        ],
        out_shape=[
            jax.ShapeDtypeStruct((seq_len, bt, d), F32),
            jax.ShapeDtypeStruct((bt, ns), F32),
            jax.ShapeDtypeStruct((bt, ns), F32),
        ],
        scratch_shapes=[
            pltpu.VMEM((tl * SUBLANES, 2 * ns), F32),
            pltpu.VMEM((SUBLANES, ns), F32),
            pltpu.VMEM((SUBLANES, ns), F32),
        ],
        compiler_params=_cparams(("arbitrary", "arbitrary"), 48),
        name="s5_scan",
    )(x_t, g.reshape(1, d), bbd, cbd, a_re, a_im, d_skip.reshape(1, d), h0_re, h0_im)


def _glu_kernel(z_ref, w_ref, x_ref, o_ref):
    zw = _dot(z_ref[...].astype(BF16), w_ref[...])
    d = x_ref.shape[1]
    o_ref[...] = x_ref[...] + zw[:, :d] * jax.nn.sigmoid(zw[:, d:])


def _glu_residual(z, w, x):
    m, d = x.shape
    tm = _row_tile(m, 512)
    return pl.pallas_call(
        _glu_kernel,
        grid=(m // tm,),
        in_specs=[
            pl.BlockSpec((tm, d), lambda i: (i, 0)),
            pl.BlockSpec((d, 2 * d), lambda i: (0, 0)),
            pl.BlockSpec((tm, d), lambda i: (i, 0)),
        ],
        out_specs=pl.BlockSpec((tm, d), lambda i: (i, 0)),
        out_shape=jax.ShapeDtypeStruct((m, d), F32),
        compiler_params=_cparams(("arbitrary",), 48),
        name="glu_residual",
    )(z, w, x)


def _s5_params(a_re, a_im, b_re, b_im, c_re, c_im, log_dt, n_gc):
    groups, n_state = a_re.shape
    gsz = b_re.shape[2]
    a = lax.complex(a_re.astype(F32), a_im.astype(F32))
    dt = jnp.exp(log_dt.astype(F32))[:, None]
    a_bar = jnp.exp(a * dt)
    b_bar = ((a_bar - 1.0) / a)[..., None] * lax.complex(b_re.astype(F32), b_im.astype(F32))
    gpc = groups // n_gc
    eye = jnp.eye(gpc, dtype=F32)

    def in_blockdiag(m):
        m = m.reshape(n_gc, gpc, n_state, gsz).transpose(0, 1, 3, 2)
        return (m[:, :, :, None, :] * eye[None, :, None, :, None]).reshape(n_gc, gpc * gsz, gpc * n_state)

    def out_blockdiag(m):
        m = m.reshape(n_gc, gpc, gsz, n_state).transpose(0, 1, 3, 2)
        return (m[:, :, :, None, :] * eye[None, :, None, :, None]).reshape(n_gc, gpc * n_state, gpc * gsz)

    bbd = jnp.concatenate([in_blockdiag(b_bar.real), in_blockdiag(b_bar.imag)], axis=2).astype(BF16)
    cbd = jnp.concatenate([out_blockdiag(c_re.astype(F32)), -out_blockdiag(c_im.astype(F32))], axis=1).astype(BF16)
    return bbd, cbd, a_bar.real.reshape(1, -1), a_bar.imag.reshape(1, -1)


def kernel(x_prompt, x_sample, mem_prompt, cache_fox_k, cache_fox_v, cache_fox_logf, state_swa_k, state_swa_v, state_ssm_re, state_ssm_im, cache_mem_k, cache_mem_v, page_table, norm_g, final_g, w_ffn_gate, w_ffn_up, w_ffn_down, w_xq, w_xkv, w_xo, w_fox_qkvf, b_fox_f, w_fox_o, w_swa_qkv, swa_sink, w_swa_o, ssm_a_re, ssm_a_im, ssm_b_re, ssm_b_im, ssm_c_re, ssm_c_im, ssm_d, ssm_log_dt, w_ssm_glu):
    batch, seq, d = x_prompt.shape
    n_seq, n_tok, _ = x_sample.shape
    depth = norm_g.shape[0]
    n_mem = mem_prompt.shape[1]
    fox_kvh, fox_hd = cache_fox_k.shape[3], cache_fox_k.shape[4]
    fox_heads = cache_fox_logf.shape[3]
    fox_qd, fox_kd = fox_heads * fox_hd, fox_kvh * fox_hd
    swa_kvh, swa_hd = state_swa_k.shape[3], state_swa_k.shape[4]
    swa_heads = swa_sink.shape[1]
    swa_qd, swa_kd = swa_heads * swa_hd, swa_kvh * swa_hd
    swa_group = swa_heads // swa_kvh
    window = state_swa_k.shape[2]
    x_heads, x_hd = cache_mem_k.shape[3], cache_mem_k.shape[4]
    xd = x_heads * x_hd
    past_len = page_table.shape[1] * cache_fox_k.shape[2]
    n_groups, n_state = ssm_a_re.shape[1], ssm_a_re.shape[2]

    yp = x_prompt.reshape(batch * seq, d)
    ys = x_sample.reshape(n_seq * n_tok, d)
    mem2d = mem_prompt.reshape(batch * n_mem, d)

    wg = w_ffn_gate.astype(BF16)
    wu = w_ffn_up.astype(BF16)
    wd = w_ffn_down.astype(BF16)
    wxq = w_xq.astype(BF16)
    wxkv = w_xkv.astype(BF16)
    wxo = w_xo.astype(BF16)
    wfox = jnp.pad(w_fox_qkvf, ((0, 0), (0, 0), (0, LANES - fox_heads))).astype(BF16)
    bfox = jnp.pad(b_fox_f, ((0, 0), (0, LANES - fox_heads))).astype(F32)
    wfoxo = w_fox_o.astype(BF16)
    wswa = w_swa_qkv.astype(BF16)
    wswao = w_swa_o.astype(BF16)
    wglu = w_ssm_glu.astype(BF16)

    half = swa_hd // 2
    inv_freq = ROPE_THETA ** (-jnp.arange(half, dtype=F32) / half)

    def rope_tables(pos, reps):
        ang = pos.astype(F32)[:, None] * inv_freq[None, :]
        cos = jnp.cos(ang)
        sin = jnp.sin(ang)
        cos_t = jnp.tile(jnp.concatenate([cos, cos], axis=1), (reps, LANES // swa_hd))
        sin_t = jnp.tile(jnp.concatenate([-sin, sin], axis=1), (reps, LANES // swa_hd))
        return cos_t, sin_t

    fox_p, fox_s, swa_p, swa_s, ssm_p, ssm_s, mem_p = [], [], [], [], [], [], []
    for l in range(depth):
        kind, j = l % 3, l // 3
        yp = _ffn(yp, norm_g[l, 0], wg[l, 0], wu[l, 0], wd[l, 0])
        ys = _ffn(ys, norm_g[l, 0], wg[l, 0], wu[l, 0], wd[l, 0])
        if kind == 0:
            qp, kp, vp, lfp, cp = _fox_project(yp, norm_g[l, 1], wfox[j], bfox[j:j + 1], seq,
                                               qd=fox_qd, kd=fox_kd, nh=fox_heads, hd=fox_hd, q_dtype=BF16)
            op = _fox_prompt_attention(qp, kp, vp, cp, batch, seq, fox_heads, fox_kvh, fox_hd)
            yp = _matmul_residual(op, wfoxo[j], yp)
            qs, ks, vs, lfs, _ = _fox_project(ys, norm_g[l, 1], wfox[j], bfox[j:j + 1], n_seq * n_tok,
                                              qd=fox_qd, kd=fox_kd, nh=fox_heads, hd=fox_hd, q_dtype=F32)
            os_ = _fox_sample_attention(qs, ks, vs, lfs, cache_fox_k, cache_fox_v, cache_fox_logf, j,
                                        page_table, n_seq, n_tok, fox_heads, fox_kvh, fox_hd)
            ys = _matmul_residual(os_, wfoxo[j], ys)
            fox_p.append((kp.reshape(batch, seq, fox_kvh, fox_hd), vp.reshape(batch, seq, fox_kvh, fox_hd),
                          lfp.reshape(batch, seq, fox_heads)))
            fox_s.append((ks.reshape(n_seq, n_tok, fox_kvh, fox_hd), vs.reshape(n_seq, n_tok, fox_kvh, fox_hd),
                          lfs.reshape(n_seq, n_tok, fox_heads)))
        elif kind == 1:
            sink_rows = jnp.broadcast_to(swa_sink[j].astype(F32).reshape(1, swa_kvh, swa_group, 1, 1),
                                         (1, swa_kvh, swa_group, window, 1))
            cos_p, sin_p = rope_tables(jnp.arange(seq), batch)
            qp, kp, vp = _swa_project(yp, norm_g[l, 1], wswa[j], cos_p, sin_p,
                                      qd=swa_qd, kd=swa_kd, hd=swa_hd, q_dtype=BF16)
            nb = seq // window
            q_r = _group_rows(qp, batch, nb, window, swa_kvh, swa_group, swa_hd)
            k_r = kp.reshape(batch, seq, swa_kvh, swa_hd).transpose(0, 2, 1, 3)
            v_r = vp.reshape(batch, seq, swa_kvh, swa_hd).transpose(0, 2, 1, 3)
            o_r = _local_attention(q_r, k_r, v_r, k_r, v_r, sink_rows.reshape(1, swa_kvh, swa_group * window, 1),
                                   tq=window, window=window, prev_is_cur=True, out_dtype=BF16)
            op = _ungroup_rows(o_r, batch, nb, window, swa_kvh, swa_group, swa_hd)
            yp = _matmul_residual(op, wswao[j], yp)
            k4 = kp.reshape(batch, seq, swa_kvh, swa_hd)
            v4 = vp.reshape(batch, seq, swa_kvh, swa_hd)
            n_keep = min(window, seq)
            swa_p.append((k4[:, seq - n_keep:], v4[:, seq - n_keep:]))
            cos_s, sin_s = rope_tables(past_len + jnp.arange(n_tok), n_seq)
            qs, ks, vs = _swa_project(ys, norm_g[l, 1], wswa[j], cos_s, sin_s,
                                      qd=swa_qd, kd=swa_kd, hd=swa_hd, q_dtype=F32)
            q_r = _group_rows(qs, n_seq, 1, n_tok, swa_kvh, swa_group, swa_hd)
            kb = state_swa_k[j].transpose(0, 2, 1, 3)
            vb = state_swa_v[j].transpose(0, 2, 1, 3)
            k_r = ks.reshape(n_seq, n_tok, swa_kvh, swa_hd).transpose(0, 2, 1, 3)
            v_r = vs.reshape(n_seq, n_tok, swa_kvh, swa_hd).transpose(0, 2, 1, 3)
            sink_s = sink_rows[:, :, :, :n_tok].reshape(1, swa_kvh, swa_group * n_tok, 1)
            o_r = _local_attention(q_r, kb, vb, k_r, v_r, sink_s, tq=n_tok, window=window, prev_is_cur=False, out_dtype=F32)
            os_ = _ungroup_rows(o_r, n_seq, 1, n_tok, swa_kvh, swa_group, swa_hd)
            ys = _matmul_residual(os_, wswao[j], ys)
            kk = jnp.concatenate([state_swa_k[j], ks.reshape(n_seq, n_tok, swa_kvh, swa_hd)], axis=1)
            vv = jnp.concatenate([state_swa_v[j], vs.reshape(n_seq, n_tok, swa_kvh, swa_hd)], axis=1)
            swa_s.append((kk[:, -window:], vv[:, -window:]))
        else:
            n_gc = 4
            bbd, cbd, a_re, a_im = _s5_params(ssm_a_re[j], ssm_a_im[j], ssm_b_re[j], ssm_b_im[j],
                                              ssm_c_re[j], ssm_c_im[j], ssm_log_dt[j], n_gc)
            ns = n_groups * n_state

            def run(y2d, nbatch, nlen, h0_re, h0_im):
                x_t = y2d.reshape(nbatch, nlen, d).transpose(1, 0, 2)
                z_t, l_re, l_im = _s5_scan(x_t, norm_g[l, 1], bbd, cbd, a_re, a_im, ssm_d[j], h0_re, h0_im)
                out_t = _glu_residual(z_t.reshape(nlen * nbatch, d), wglu[j], x_t.reshape(nlen * nbatch, d))
                y_new = out_t.reshape(nlen, nbatch, d).transpose(1, 0, 2).reshape(nbatch * nlen, d)
                return y_new, l_re.reshape(nbatch, n_groups, n_state), l_im.reshape(nbatch, n_groups, n_state)

            zeros = jnp.zeros((batch, ns), F32)
            yp, rp, ip = run(yp, batch, seq, zeros, zeros)
            ys, rn, inn = run(ys, n_seq, n_tok, state_ssm_re[j].astype(F32).reshape(n_seq, ns),
                              state_ssm_im[j].astype(F32).reshape(n_seq, ns))
            ssm_p.append((rp, ip))
            ssm_s.append((rn, inn))
        kvm = _norm_matmul(mem2d, norm_g[l, 2], wxkv[l], F32)
        mem_p.append((kvm[:, :xd].reshape(batch, n_mem, x_heads, x_hd), kvm[:, xd:].reshape(batch, n_mem, x_heads, x_hd)))
        kvm4 = kvm.reshape(1, batch, n_mem, 2 * xd)
        qx = _norm_matmul(yp, norm_g[l, 3], wxq[l], BF16)
        ox = _cross_attention(qx, kvm4, kvm4, 0, 1, 0, batch, seq, x_heads, x_hd, BF16)
        yp = _matmul_residual(ox, wxo[l], yp)
        qx = _norm_matmul(ys, norm_g[l, 3], wxq[l], F32)
        ox = _cross_attention(qx, cache_mem_k.reshape(depth, n_seq, n_mem, xd), cache_mem_v.reshape(depth, n_seq, n_mem, xd),
                              0, 0, l, n_seq, n_tok, x_heads, x_hd, F32)
        ys = _matmul_residual(ox, wxo[l], ys)
        yp = _ffn(yp, norm_g[l, 4], wg[l, 1], wu[l, 1], wd[l, 1])
        ys = _ffn(ys, norm_g[l, 4], wg[l, 1], wu[l, 1], wd[l, 1])

    y_prompt = _final_norm(yp, final_g).reshape(batch, seq, d)
    y_sample = _final_norm(ys, final_g).reshape(n_seq, n_tok, d)

    def stack(entries, i):
        return jnp.stack([e[i] for e in entries])

    return (y_prompt, y_sample,
            stack(fox_p, 0), stack(fox_p, 1), stack(fox_p, 2),
            stack(swa_p, 0), stack(swa_p, 1),
            stack(ssm_p, 0), stack(ssm_p, 1),
            stack(mem_p, 0), stack(mem_p, 1),
            stack(fox_s, 0), stack(fox_s, 1), stack(fox_s, 2),
            stack(swa_s, 0), stack(swa_s, 1),
            stack(ssm_s, 0), stack(ssm_s, 1))
```

```python
import functools
import math

import jax
import jax.numpy as jnp
from jax import lax
from jax.experimental import pallas as pl
from jax.experimental.pallas import tpu as pltpu

F32 = jnp.float32
BF16 = jnp.bfloat16

RMS_EPS = 1e-6
NEG_INF = -1e30
ROPE_THETA = 10000.0
SSM_GROUP = 16
GELU_C = math.sqrt(2.0 / math.pi)

V7X_VMEM_BYTES = 64 * 1024 * 1024
LANES = 128
SUBLANES = 8


def _cparams(sem, vmem_mb):
    assert vmem_mb * 2**20 < V7X_VMEM_BYTES
    return pltpu.CompilerParams(dimension_semantics=sem, vmem_limit_bytes=vmem_mb * 2**20)


def _rms(x, g):
    ms = jnp.mean(x * x, axis=-1, keepdims=True)
    return (x * lax.rsqrt(ms + RMS_EPS)) * g


def _dot(a, b):
    return jnp.dot(a, b, preferred_element_type=F32)


def _dot_nt(a, b):
    return lax.dot_general(a, b, (((1,), (1,)), ((), ())), preferred_element_type=F32)


def _dot_exact(a, b):
    return jnp.dot(a, b, preferred_element_type=F32, precision=lax.Precision.HIGHEST)


def _row_tile(m, want):
    t = min(want, m)
    assert m % t == 0, (m, t)
    return t


def _ffn_kernel(x_ref, g_ref, wg_ref, wu_ref, wd_ref, o_ref, *, fc):
    x = x_ref[...]
    h = _rms(x, g_ref[...]).astype(BF16)
    acc = None
    for c in range(wg_ref.shape[1] // fc):
        gate = _dot(h, wg_ref[:, c * fc:(c + 1) * fc])
        up = _dot(h, wu_ref[:, c * fc:(c + 1) * fc])
        act = (gate * jax.nn.sigmoid(gate) * up).astype(BF16)
        part = _dot(act, wd_ref[c * fc:(c + 1) * fc, :])
        acc = part if acc is None else acc + part
    o_ref[...] = x + 0.5 * acc


def _ffn(x, g, wg, wu, wd):
    m, d = x.shape
    ff = wg.shape[1]
    tm = _row_tile(m, 256)
    fc = 256
    assert ff % fc == 0
    return pl.pallas_call(
        functools.partial(_ffn_kernel, fc=fc),
        grid=(m // tm,),
        in_specs=[
            pl.BlockSpec((tm, d), lambda i: (i, 0)),
            pl.BlockSpec((1, d), lambda i: (0, 0)),
            pl.BlockSpec((d, ff), lambda i: (0, 0)),
            pl.BlockSpec((d, ff), lambda i: (0, 0)),
            pl.BlockSpec((ff, d), lambda i: (0, 0)),
        ],
        out_specs=pl.BlockSpec((tm, d), lambda i: (i, 0)),
        out_shape=jax.ShapeDtypeStruct((m, d), F32),
        compiler_params=_cparams(("arbitrary",), 56),
        name="ffn",
    )(x, g.reshape(1, d), wg, wu, wd)


def _nmm_kernel(x_ref, g_ref, w_ref, o_ref):
    h = _rms(x_ref[...], g_ref[...]).astype(BF16)
    o_ref[...] = _dot(h, w_ref[...]).astype(o_ref.dtype)


def _norm_matmul(x, g, w, out_dtype):
    m, d = x.shape
    n = w.shape[1]
    tm = _row_tile(m, 512)
    return pl.pallas_call(
        _nmm_kernel,
        grid=(m // tm,),
        in_specs=[
            pl.BlockSpec((tm, d), lambda i: (i, 0)),
            pl.BlockSpec((1, d), lambda i: (0, 0)),
            pl.BlockSpec((d, n), lambda i: (0, 0)),
        ],
        out_specs=pl.BlockSpec((tm, n), lambda i: (i, 0)),
        out_shape=jax.ShapeDtypeStruct((m, n), out_dtype),
        compiler_params=_cparams(("arbitrary",), 40),
        name="norm_matmul",
    )(x, g.reshape(1, d), w)


def _mmres_kernel(a_ref, w_ref, x_ref, o_ref):
    o_ref[...] = x_ref[...] + _dot(a_ref[...].astype(BF16), w_ref[...])


def _matmul_residual(a, w, x):
    m, k = a.shape
    d = w.shape[1]
    tm = _row_tile(m, 512)
    return pl.pallas_call(
        _mmres_kernel,
        grid=(m // tm,),
        in_specs=[
            pl.BlockSpec((tm, k), lambda i: (i, 0)),
            pl.BlockSpec((k, d), lambda i: (0, 0)),
            pl.BlockSpec((tm, d), lambda i: (i, 0)),
        ],
        out_specs=pl.BlockSpec((tm, d), lambda i: (i, 0)),
        out_shape=jax.ShapeDtypeStruct((m, d), F32),
        compiler_params=_cparams(("arbitrary",), 40),
        name="matmul_residual",
    )(a, w, x)


def _final_norm_kernel(x_ref, g_ref, o_ref):
    o_ref[...] = _rms(x_ref[...], g_ref[...])


def _final_norm(x, g):
    m, d = x.shape
    tm = _row_tile(m, 1024)
    return pl.pallas_call(
        _final_norm_kernel,
        grid=(m // tm,),
        in_specs=[pl.BlockSpec((tm, d), lambda i: (i, 0)), pl.BlockSpec((1, d), lambda i: (0, 0))],
        out_specs=pl.BlockSpec((tm, d), lambda i: (i, 0)),
        out_shape=jax.ShapeDtypeStruct((m, d), F32),
        compiler_params=_cparams(("arbitrary",), 40),
        name="final_norm",
    )(x, g.reshape(1, d))


def _fox_proj_kernel(x_ref, g_ref, w_ref, b_ref, q_ref, k_ref, v_ref, lf_ref, c_ref, carry_ref,
                     *, qd, kd, nh, tiles_per_seq, q_scale):
    i = pl.program_id(0)
    h = _rms(x_ref[...], g_ref[...]).astype(BF16)
    proj = _dot(h, w_ref[...])
    q_ref[...] = (proj[:, :qd] * q_scale).astype(q_ref.dtype)
    k_ref[...] = proj[:, qd:qd + kd]
    v_ref[...] = proj[:, qd + kd:qd + 2 * kd]
    z = proj[:, qd + 2 * kd:] + b_ref[...]
    lf = jnp.minimum(z, 0.0) - jnp.log1p(jnp.exp(-jnp.abs(z)))
    lf_ref[...] = lf[:, :nh]

    @pl.when(i % tiles_per_seq == 0)
    def _():
        carry_ref[...] = jnp.zeros_like(carry_ref)

    tm = lf.shape[0]
    r = lax.broadcasted_iota(jnp.int32, (tm, tm), 0)
    c = lax.broadcasted_iota(jnp.int32, (tm, tm), 1)
    tril = jnp.where(c <= r, 1.0, 0.0).astype(F32)
    csum = _dot_exact(tril, lf) + carry_ref[...]
    c_ref[...] = csum[:, :nh]
    carry_ref[...] = csum[tm - 1:tm, :]


def _fox_project(x, g, w_pad, b_pad, seq_len, *, qd, kd, nh, hd, q_dtype):
    m, d = x.shape
    n = w_pad.shape[1]
    tm = _row_tile(seq_len, 512)
    kern = functools.partial(_fox_proj_kernel, qd=qd, kd=kd, nh=nh, tiles_per_seq=seq_len // tm,
                             q_scale=hd ** -0.5)
    return pl.pallas_call(
        kern,
        grid=(m // tm,),
        in_specs=[
            pl.BlockSpec((tm, d), lambda i: (i, 0)),
            pl.BlockSpec((1, d), lambda i: (0, 0)),
            pl.BlockSpec((d, n), lambda i: (0, 0)),
            pl.BlockSpec((1, LANES), lambda i: (0, 0)),
        ],
        out_specs=[
            pl.BlockSpec((tm, qd), lambda i: (i, 0)),
            pl.BlockSpec((tm, kd), lambda i: (i, 0)),
            pl.BlockSpec((tm, kd), lambda i: (i, 0)),
            pl.BlockSpec((tm, nh), lambda i: (i, 0)),
            pl.BlockSpec((tm, nh), lambda i: (i, 0)),
        ],
        out_shape=[
            jax.ShapeDtypeStruct((m, qd), q_dtype),
            jax.ShapeDtypeStruct((m, kd), F32),
            jax.ShapeDtypeStruct((m, kd), F32),
            jax.ShapeDtypeStruct((m, nh), F32),
            jax.ShapeDtypeStruct((m, nh), F32),
        ],
        scratch_shapes=[pltpu.VMEM((1, LANES), F32)],
        compiler_params=_cparams(("arbitrary",), 48),
        name="fox_project",
    )(x, g.reshape(1, d), w_pad, b_pad)


def _split3(x):
    a = lax.reduce_precision(x, exponent_bits=8, mantissa_bits=7)
    r = x - a
    b = lax.reduce_precision(r, exponent_bits=8, mantissa_bits=7)
    return a.astype(BF16), b.astype(BF16), (r - b).astype(BF16)


def _fox_prompt_kernel(q_ref, k_ref, v_ref, o_ref, *, tq):
    i = pl.program_id(2)
    q = q_ref[0, 0, 0]
    rows = q.shape[0]

    def step(j, carry, masked):
        m, l, acc = carry
        start = pl.multiple_of(j * tq, tq)
        k_j = k_ref[0, 0, pl.ds(start, tq), :]
        v_j = v_ref[0, 0, pl.ds(start, tq), :]
        s = _dot_nt(q, k_j)
        if masked:
            r = lax.broadcasted_iota(jnp.int32, (rows, tq), 0) % tq
            cc = lax.broadcasted_iota(jnp.int32, (rows, tq), 1)
            s = jnp.where(cc <= r, s, NEG_INF)
        m_new = jnp.maximum(m, jnp.max(s, axis=1, keepdims=True))
        alpha = jnp.exp(m - m_new)
        p = jnp.exp(s - m_new)
        l = alpha * l + jnp.sum(p, axis=1, keepdims=True)
        acc = alpha * acc + _dot(p.astype(BF16), v_j)
        return m_new, l, acc

    hd = v_ref.shape[3]
    init = (jnp.full((rows, 1), NEG_INF, F32), jnp.zeros((rows, 1), F32), jnp.zeros((rows, hd), F32))
    carry = lax.fori_loop(0, i, lambda j, cr: step(j, cr, False), init)
    m, l, acc = step(i, carry, True)
    o_ref[0, 0, 0] = (acc / l).astype(o_ref.dtype)


def _fox_prompt_attention(q, k, v, c, batch, seq_len, n_heads, kv_heads, hd):
    group = n_heads // kv_heads
    tq = _row_tile(seq_len, 256)
    nq = seq_len // tq
    nterm = 3
    assert hd + 2 * nterm * group <= LANES
    c1, c2, c3 = _split3(c.reshape(batch, seq_len, kv_heads, group))
    one = jnp.ones_like(c1)
    eye = jnp.eye(group, dtype=BF16)
    q_bias = jnp.stack([c1, c2, c3, one, one, one], axis=-1)
    q_bias = (q_bias[:, :, :, :, None, :] * eye[None, None, None, :, :, None]).reshape(
        batch, seq_len, kv_heads, group, 2 * nterm * group)
    k_bias = jnp.stack([one, one, one, -c1, -c2, -c3], axis=-1).reshape(
        batch, seq_len, kv_heads, 2 * nterm * group)
    pad = LANES - hd - 2 * nterm * group
    q_aug = jnp.concatenate([q.reshape(batch, seq_len, kv_heads, group, hd), q_bias,
                             jnp.zeros((batch, seq_len, kv_heads, group, pad), BF16)], axis=-1)
    k_aug = jnp.concatenate([k.astype(BF16).reshape(batch, seq_len, kv_heads, hd), k_bias,
                             jnp.zeros((batch, seq_len, kv_heads, pad), BF16)], axis=-1)
    q_r = (q_aug.reshape(batch, nq, tq, kv_heads, group, LANES).transpose(0, 3, 1, 4, 2, 5)
           .reshape(batch, kv_heads, nq, group * tq, LANES))
    k_r = k_aug.transpose(0, 2, 1, 3)
    v_r = v.astype(BF16).reshape(batch, seq_len, kv_heads, hd).transpose(0, 2, 1, 3)
    kern = functools.partial(_fox_prompt_kernel, tq=tq)
    o = pl.pallas_call(
        kern,
        grid=(batch, kv_heads, nq),
        in_specs=[
            pl.BlockSpec((1, 1, 1, group * tq, LANES), lambda b, kv, i: (b, kv, i, 0, 0)),
            pl.BlockSpec((1, 1, seq_len, LANES), lambda b, kv, i: (b, kv, 0, 0)),
            pl.BlockSpec((1, 1, seq_len, hd), lambda b, kv, i: (b, kv, 0, 0)),
        ],
        out_specs=pl.BlockSpec((1, 1, 1, group * tq, hd), lambda b, kv, i: (b, kv, i, 0, 0)),
        out_shape=jax.ShapeDtypeStruct((batch, kv_heads, nq, group * tq, hd), BF16),
        compiler_params=_cparams(("arbitrary", "arbitrary", "arbitrary"), 40),
        name="fox_prompt_attention",
    )(q_r, k_r, v_r)
    return (o.reshape(batch, kv_heads, nq, group, tq, hd).transpose(0, 2, 4, 1, 3, 5)
            .reshape(batch * seq_len, n_heads * hd))


def _fox_sample_kernel(pt_ref, q_ref, knt_ref, vnt_ref, lfr_ref, kpool, vpool, fpool, o_ref,
                       kbuf, vbuf, fbuf, sem, *, layer, cpp, n_chunks, page, n_tok, heads, hd, kv_heads):
    b = pl.program_id(0)
    nb = pl.num_programs(0)
    rows = heads * n_tok
    ckeys = cpp * page

    def copies(seq, chunk, slot):
        out = []
        for pi in range(cpp):
            pg = pt_ref[seq, chunk * cpp + pi]
            lanes_pi = pl.ds(pi * page, page)
            out.append(pltpu.make_async_copy(kpool.at[layer, pg], kbuf.at[slot, :, lanes_pi], sem.at[slot]))
            out.append(pltpu.make_async_copy(vpool.at[layer, pg], vbuf.at[slot, :, lanes_pi], sem.at[slot]))
            out.append(pltpu.make_async_copy(fpool.at[layer, pg], fbuf.at[slot, :, lanes_pi], sem.at[slot]))
        return out

    def start(seq, chunk, slot):
        for cp in copies(seq, chunk, slot):
            cp.start()

    def wait(seq, chunk, slot):
        for cp in copies(seq, chunk, slot):
            cp.wait()

    @pl.when(b == 0)
    def _():
        start(0, n_chunks - 1, 0)

    q = q_ref[0]
    lfr = lfr_ref[0]
    lane = lax.broadcasted_iota(jnp.int32, (rows, page), 1)
    trow = lax.broadcasted_iota(jnp.int32, (rows, page), 0) % n_tok
    ckn = lfr
    k = 1
    while k < n_tok:
        ckn = ckn + jnp.where(lane >= k, pltpu.roll(ckn, k, 1), 0.0)
        k *= 2
    cq = jnp.sum(jnp.where(lane == trow, ckn, 0.0), axis=1, keepdims=True)

    s = _dot(q, knt_ref[0].astype(BF16))
    s = jnp.where(lane <= trow, s + (cq - ckn), NEG_INF)
    m0 = jnp.max(s, axis=1, keepdims=True)
    p = jnp.exp(s - m0)
    l0 = jnp.sum(p, axis=1, keepdims=True)
    acc0 = _dot_nt(p.astype(BF16), vnt_ref[0].astype(BF16))

    lane_c = lax.broadcasted_iota(jnp.int32, (heads, ckeys), 1)

    def chunk_body(ci, carry):
        m, l, acc, tail = carry
        chunk = n_chunks - 1 - ci
        g = b * n_chunks + ci
        slot = g % 2
        wait(b, chunk, slot)

        @pl.when(ci + 1 < n_chunks)
        def _():
            start(b, chunk - 1, 1 - slot)

        @pl.when(jnp.logical_and(ci + 1 == n_chunks, b + 1 < nb))
        def _():
            start(b + 1, n_chunks - 1, 1 - slot)

        f = fbuf[slot]
        r = f
        k = 1
        while k < ckeys:
            r = r + jnp.where(lane_c < ckeys - k, pltpu.roll(r, ckeys - k, 1), 0.0)
            k *= 2
        suf = (r - f) + tail
        bias = jnp.concatenate(
            [jnp.broadcast_to(suf[hh:hh + 1, :], (n_tok, ckeys)) for hh in range(heads)], axis=0)
        s = _dot(q, kbuf[slot].astype(BF16)) + (bias + cq)
        m_new = jnp.maximum(m, jnp.max(s, axis=1, keepdims=True))
        alpha = jnp.exp(m - m_new)
        p = jnp.exp(s - m_new)
        l = alpha * l + jnp.sum(p, axis=1, keepdims=True)
        acc = alpha * acc + _dot_nt(p.astype(BF16), vbuf[slot].astype(BF16))
        return m_new, l, acc, tail + r[:, 0:1]

    init = (m0, l0, acc0, jnp.zeros((heads, 1), F32))
    m, l, acc, _ = lax.fori_loop(0, n_chunks, chunk_body, init)
    o = acc / l
    rpk = rows // kv_heads
    o_ref[0] = jnp.concatenate([o[kk * rpk:(kk + 1) * rpk, kk * hd:(kk + 1) * hd] for kk in range(kv_heads)],
                               axis=0)


def _fox_sample_attention(q, k_new, v_new, logf_new, k_pools, v_pools, f_pools, layer, page_table,
                          n_seq, n_tok, n_heads, kv_heads, hd):
    group = n_heads // kv_heads
    n_layers, n_pool, page = k_pools.shape[0], k_pools.shape[1], k_pools.shape[2]
    n_pages = page_table.shape[1]
    width = kv_heads * hd
    rows = n_heads * n_tok
    cpp = 8 if n_pages % 8 == 0 else 1
    assert page % LANES == 0 and n_tok <= page
    n_chunks = n_pages // cpp
    q_t = q.reshape(n_seq, n_tok, kv_heads, group, hd).transpose(0, 2, 3, 1, 4).reshape(n_seq, kv_heads, group * n_tok, hd)
    eye = jnp.eye(kv_heads, dtype=F32)
    q_bd = (q_t[:, :, :, None, :] * eye[None, :, None, :, None]).reshape(n_seq, rows, width).astype(BF16)
    pad = ((0, 0), (0, 0), (0, page - n_tok))
    lfr = jnp.broadcast_to(logf_new.reshape(n_seq, n_tok, n_heads).transpose(0, 2, 1)[:, :, None, :],
                           (n_seq, n_heads, n_tok, n_tok)).reshape(n_seq, rows, n_tok)
    lfr = jnp.pad(lfr, pad)
    knt = jnp.pad(k_new.reshape(n_seq, n_tok, width).transpose(0, 2, 1), pad)
    vnt = jnp.pad(v_new.reshape(n_seq, n_tok, width).transpose(0, 2, 1), pad)
    kp = k_pools.transpose(0, 1, 3, 4, 2).reshape(n_layers, n_pool, width, page)
    vp = v_pools.transpose(0, 1, 3, 4, 2).reshape(n_layers, n_pool, width, page)
    fp = f_pools.transpose(0, 1, 3, 2)
    kern = functools.partial(_fox_sample_kernel, layer=layer, cpp=cpp, n_chunks=n_chunks, page=page,
                             n_tok=n_tok, heads=n_heads, hd=hd, kv_heads=kv_heads)
    grid_spec = pltpu.PrefetchScalarGridSpec(
        num_scalar_prefetch=1,
        grid=(n_seq,),
        in_specs=[
            pl.BlockSpec((1, rows, width), lambda b, pt: (b, 0, 0)),
            pl.BlockSpec((1, width, page), lambda b, pt: (b, 0, 0)),
            pl.BlockSpec((1, width, page), lambda b, pt: (b, 0, 0)),
            pl.BlockSpec((1, rows, page), lambda b, pt: (b, 0, 0)),
            pl.BlockSpec(memory_space=pl.ANY),
            pl.BlockSpec(memory_space=pl.ANY),
            pl.BlockSpec(memory_space=pl.ANY),
        ],
        out_specs=pl.BlockSpec((1, rows, hd), lambda b, pt: (b, 0, 0)),
        scratch_shapes=[
            pltpu.VMEM((2, width, cpp * page), F32),
            pltpu.VMEM((2, width, cpp * page), F32),
            pltpu.VMEM((2, n_heads, cpp * page), F32),
            pltpu.SemaphoreType.DMA((2,)),
        ],
    )
    o = pl.pallas_call(
        kern,
        grid_spec=grid_spec,
        out_shape=jax.ShapeDtypeStruct((n_seq, rows, hd), F32),
        compiler_params=_cparams(("arbitrary",), 40),
        name="fox_sample_attention",
    )(page_table, q_bd, knt, vnt, lfr, kp, vp, fp)
    return (o.reshape(n_seq, kv_heads, group, n_tok, hd).transpose(0, 3, 1, 2, 4)
            .reshape(n_seq * n_tok, n_heads * hd))


def _swa_proj_kernel(x_ref, g_ref, w_ref, cos_ref, sin_ref, q_ref, k_ref, v_ref, *, qd, kd, hd, q_scale):
    h = _rms(x_ref[...], g_ref[...]).astype(BF16)
    proj = _dot(h, w_ref[...])
    cos = cos_ref[...]
    sin = sin_ref[...]
    half = hd // 2

    def rope(x):
        n = x.shape[1]
        reps = n // LANES
        cs = jnp.concatenate([cos] * reps, axis=1) if reps > 1 else cos
        sn = jnp.concatenate([sin] * reps, axis=1) if reps > 1 else sin
        lane = lax.broadcasted_iota(jnp.int32, x.shape, 1) % hd
        up = pltpu.roll(x, n - half, 1)
        dn = pltpu.roll(x, half, 1)
        return x * cs + jnp.where(lane < half, up, dn) * sn

    q_ref[...] = (rope(proj[:, :qd]) * q_scale).astype(q_ref.dtype)
    k_ref[...] = rope(proj[:, qd:qd + kd])
    v_ref[...] = proj[:, qd + kd:qd + 2 * kd]


def _swa_project(x, g, w, cos, sin, *, qd, kd, hd, q_dtype):
    m, d = x.shape
    n = w.shape[1]
    tm = _row_tile(m, 512)
    kern = functools.partial(_swa_proj_kernel, qd=qd, kd=kd, hd=hd, q_scale=hd ** -0.5)
    return pl.pallas_call(
        kern,
        grid=(m // tm,),
        in_specs=[
            pl.BlockSpec((tm, d), lambda i: (i, 0)),
            pl.BlockSpec((1, d), lambda i: (0, 0)),
            pl.BlockSpec((d, n), lambda i: (0, 0)),
            pl.BlockSpec((tm, LANES), lambda i: (i, 0)),
            pl.BlockSpec((tm, LANES), lambda i: (i, 0)),
        ],
        out_specs=[
            pl.BlockSpec((tm, qd), lambda i: (i, 0)),
            pl.BlockSpec((tm, kd), lambda i: (i, 0)),
            pl.BlockSpec((tm, kd), lambda i: (i, 0)),
        ],
        out_shape=[
            jax.ShapeDtypeStruct((m, qd), q_dtype),
            jax.ShapeDtypeStruct((m, kd), F32),
            jax.ShapeDtypeStruct((m, kd), F32),
        ],
        compiler_params=_cparams(("arbitrary",), 48),
        name="swa_project",
    )(x, g.reshape(1, d), w, cos, sin)


def _local_attn_kernel(q_ref, kp_ref, vp_ref, kc_ref, vc_ref, sink_ref, o_ref, *, tq, window, first_block_has_prev):
    i = pl.program_id(2)
    q = q_ref[0, 0, 0].astype(BF16)
    rows = q.shape[0]
    kp = kp_ref[0, 0].astype(BF16)
    vp = vp_ref[0, 0].astype(BF16)
    kc = kc_ref[0, 0].astype(BF16)
    vc = vc_ref[0, 0].astype(BF16)
    w = kp.shape[0]
    t_p = lax.broadcasted_iota(jnp.int32, (rows, w), 0) % tq
    j_p = lax.broadcasted_iota(jnp.int32, (rows, w), 1)
    ok_p = (w + t_p - j_p) <= window
    if not first_block_has_prev:
        ok_p = jnp.logical_and(ok_p, i > 0)
    t_c = lax.broadcasted_iota(jnp.int32, (rows, tq), 0) % tq
    j_c = lax.broadcasted_iota(jnp.int32, (rows, tq), 1)
    ok_c = j_c <= t_c
    s_p = jnp.where(ok_p, _dot_nt(q, kp), NEG_INF)
    s_c = jnp.where(ok_c, _dot_nt(q, kc), NEG_INF)
    sink = sink_ref[0, 0]
    m = jnp.maximum(jnp.maximum(jnp.max(s_p, axis=1, keepdims=True), jnp.max(s_c, axis=1, keepdims=True)), sink)
    p_p = jnp.exp(s_p - m)
    p_c = jnp.exp(s_c - m)
    l = jnp.sum(p_p, axis=1, keepdims=True) + jnp.sum(p_c, axis=1, keepdims=True) + jnp.exp(sink - m)
    acc = _dot(p_p.astype(BF16), vp) + _dot(p_c.astype(BF16), vc)
    o_ref[0, 0, 0] = (acc / l).astype(o_ref.dtype)


def _local_attention(q_r, k_prev, v_prev, k_cur, v_cur, sink_rows, *, tq, window, prev_is_cur, out_dtype):
    b, kvh, nb, rows, hd = q_r.shape
    w = tq if prev_is_cur else k_prev.shape[2]
    if prev_is_cur:
        prev_map = lambda bb, kv, i: (bb, kv, jnp.maximum(i - 1, 0), 0)
    else:
        assert nb == 1
        prev_map = lambda bb, kv, i: (bb, kv, 0, 0)
    kern = functools.partial(_local_attn_kernel, tq=tq, window=window, first_block_has_prev=not prev_is_cur)
    return pl.pallas_call(
        kern,
        grid=(b, kvh, nb),
        in_specs=[
            pl.BlockSpec((1, 1, 1, rows, hd), lambda bb, kv, i: (bb, kv, i, 0, 0)),
            pl.BlockSpec((1, 1, w, hd), prev_map),
            pl.BlockSpec((1, 1, w, hd), prev_map),
            pl.BlockSpec((1, 1, tq, hd), lambda bb, kv, i: (bb, kv, i, 0)),
            pl.BlockSpec((1, 1, tq, hd), lambda bb, kv, i: (bb, kv, i, 0)),
            pl.BlockSpec((1, 1, rows, 1), lambda bb, kv, i: (0, kv, 0, 0)),
        ],
        out_specs=pl.BlockSpec((1, 1, 1, rows, hd), lambda bb, kv, i: (bb, kv, i, 0, 0)),
        out_shape=jax.ShapeDtypeStruct((b, kvh, nb, rows, hd), out_dtype),
        compiler_params=_cparams(("arbitrary", "arbitrary", "arbitrary"), 40),
        name="local_attention",
    )(q_r, k_prev, v_prev, k_cur, v_cur, sink_rows)


def _group_rows(q, batch, nb, tq, kv_heads, group, hd):
    return (q.reshape(batch, nb, tq, kv_heads, group, hd).transpose(0, 3, 1, 4, 2, 5)
            .reshape(batch, kv_heads, nb, group * tq, hd))


def _ungroup_rows(o, batch, nb, tq, kv_heads, group, hd):
    return (o.reshape(batch, kv_heads, nb, group, tq, hd).transpose(0, 2, 4, 1, 3, 5)
            .reshape(batch * nb * tq, kv_heads * group * hd))


def _cross_attn_kernel(q_ref, k_ref, v_ref, o_ref, *, heads, hd, scale, head_rows):
    outs = []
    for h in range(heads):
        q = q_ref[0, :, h * hd:(h + 1) * hd].astype(BF16)
        if head_rows:
            n_mem = k_ref.shape[2] // heads
            k = k_ref[0, 0, pl.ds(h, n_mem, stride=heads), :].astype(BF16)
            v = v_ref[0, 0, pl.ds(h, n_mem, stride=heads), :].astype(BF16)
        else:
            k = k_ref[0, 0, :, h * hd:(h + 1) * hd].astype(BF16)
            v = v_ref[0, 0, :, h * hd:(h + 1) * hd].astype(BF16)
        s = _dot_nt(q, k) * scale
        m = jnp.max(s, axis=1, keepdims=True)
        p = jnp.exp(s - m)
        l = jnp.sum(p, axis=1, keepdims=True)
        outs.append(_dot(p.astype(BF16), v) / l)
    o_ref[0] = jnp.concatenate(outs, axis=1).astype(o_ref.dtype)


def _cross_attention(q, mk, mv, k_col, v_col, layer, batch, seq_len, heads, hd, out_dtype, head_rows=False):
    width = heads * hd
    tq = _row_tile(seq_len, 512)
    nq = seq_len // tq
    kv_block = (1, 1, mk.shape[2], hd if head_rows else width)
    kern = functools.partial(_cross_attn_kernel, heads=heads, hd=hd, scale=hd ** -0.5, head_rows=head_rows)
    o = pl.pallas_call(
        kern,
        grid=(batch, nq),
        in_specs=[
            pl.BlockSpec((1, tq, width), lambda b, i: (b, i, 0)),
            pl.BlockSpec(kv_block, lambda b, i: (layer, b, 0, k_col)),
            pl.BlockSpec(kv_block, lambda b, i: (layer, b, 0, v_col)),
        ],
        out_specs=pl.BlockSpec((1, tq, width), lambda b, i: (b, i, 0)),
        out_shape=jax.ShapeDtypeStruct((batch, seq_len, width), out_dtype),
        compiler_params=_cparams(("arbitrary", "arbitrary"), 40),
        name="cross_attention",
    )(q.reshape(batch, seq_len, width), mk, mv)
    return o.reshape(batch * seq_len, width)


def _s5_kernel(x_ref, g_ref, bbd_ref, cbd_ref, are_ref, aim_ref, d_ref, h0re_ref, h0im_ref,
               z_ref, lre_ref, lim_ref, st_ref, sre_ref, sim_ref, *, tl, n_gc, gc_in, gc_st, cw):
    li = pl.program_id(1)

    @pl.when(li == 0)
    def _():
        sre_ref[...] = h0re_ref[...]
        sim_ref[...] = h0im_ref[...]

    d_model = x_ref.shape[2]
    rows = tl * SUBLANES
    x = x_ref[...].reshape(rows, d_model)
    h = _rms(x, g_ref[...])
    hb = h.astype(BF16)
    for k in range(n_gc):
        st_ref[:, k * 2 * gc_st:(k + 1) * 2 * gc_st] = _dot(hb[:, k * gc_in:(k + 1) * gc_in], bbd_ref[k])

    for k in range(n_gc):
        for c in range(gc_st // cw):
            re0 = k * 2 * gc_st + c * cw
            im0 = re0 + gc_st
            sc = k * gc_st + c * cw
            ar = jnp.broadcast_to(are_ref[:, sc:sc + cw], (SUBLANES, cw))
            ai = jnp.broadcast_to(aim_ref[:, sc:sc + cw], (SUBLANES, cw))

            def body(l, carry, re0=re0, im0=im0, ar=ar, ai=ai):
                sr, si = carry
                r0 = pl.multiple_of(l * SUBLANES, SUBLANES)
                br = st_ref[pl.ds(r0, SUBLANES), re0:re0 + cw]
                bi = st_ref[pl.ds(r0, SUBLANES), im0:im0 + cw]
                nr = ar * sr - ai * si + br
                ni = ar * si + ai * sr + bi
                st_ref[pl.ds(r0, SUBLANES), re0:re0 + cw] = nr
                st_ref[pl.ds(r0, SUBLANES), im0:im0 + cw] = ni
                return nr, ni

            sr, si = lax.fori_loop(0, tl, body, (sre_ref[:, sc:sc + cw], sim_ref[:, sc:sc + cw]))
            sre_ref[:, sc:sc + cw] = sr
            sim_ref[:, sc:sc + cw] = si

    ys = [_dot(st_ref[:, k * 2 * gc_st:(k + 1) * 2 * gc_st].astype(BF16), cbd_ref[k]) for k in range(n_gc)]
    y = jnp.concatenate(ys, axis=1) + d_ref[...] * h
    z = y * (0.5 * (1.0 + jnp.tanh(GELU_C * (y + 0.044715 * (y * y * y)))))
    z_ref[...] = z.reshape(tl, SUBLANES, d_model)

    @pl.when(li == pl.num_programs(1) - 1)
    def _():
        lre_ref[...] = sre_ref[...]
        lim_ref[...] = sim_ref[...]


def _s5_scan(x_t, g, bbd, cbd, a_re, a_im, d_skip, h0_re, h0_im):
    seq_len, bt, d = x_t.shape
    n_gc, gc_in, two_gc_st = bbd.shape
    gc_st = two_gc_st // 2
    ns = n_gc * gc_st
    tl = _row_tile(seq_len, 32)
    kern = functools.partial(_s5_kernel, tl=tl, n_gc=n_gc, gc_in=gc_in, gc_st=gc_st, cw=512)
    return pl.pallas_call(
        kern,
        grid=(bt // SUBLANES, seq_len // tl),
        in_specs=[
            pl.BlockSpec((tl, SUBLANES, d), lambda b, l: (l, b, 0)),
            pl.BlockSpec((1, d), lambda b, l: (0, 0)),
            pl.BlockSpec((n_gc, gc_in, 2 * gc_st), lambda b, l: (0, 0, 0)),
            pl.BlockSpec((n_gc, 2 * gc_st, gc_in), lambda b, l: (0, 0, 0)),
            pl.BlockSpec((1, ns), lambda b, l: (0, 0)),
            pl.BlockSpec((1, ns), lambda b, l: (0, 0)),
            pl.BlockSpec((1, d), lambda b, l: (0, 0)),
            pl.BlockSpec((SUBLANES, ns), lambda b, l: (b, 0)),
            pl.BlockSpec((SUBLANES, ns), lambda b, l: (b, 0)),
        ],
        out_specs=[
            pl.BlockSpec((tl, SUBLANES, d), lambda b, l: (l, b, 0)),
            pl.BlockSpec((SUBLANES, ns), lambda b, l: (b, 0)),
            pl.BlockSpec((SUBLANES, ns), lambda b, l: (b, 0)),
        ],
        out_shape=[
            jax.ShapeDtypeStruct((seq_len, bt, d), F32),
            jax.ShapeDtypeStruct((bt, ns), F32),
            jax.ShapeDtypeStruct((bt, ns), F32),
        ],
        scratch_shapes=[
            pltpu.VMEM((tl * SUBLANES, 2 * ns), F32),
            pltpu.VMEM((SUBLANES, ns), F32),
            pltpu.VMEM((SUBLANES, ns), F32),
        ],
        compiler_params=_cparams(("arbitrary", "arbitrary"), 48),
        name="s5_scan",
    )(x_t, g.reshape(1, d), bbd, cbd, a_re, a_im, d_skip.reshape(1, d), h0_re, h0_im)


def _glu_kernel(z_ref, w_ref, x_ref, o_ref):
    zw = _dot(z_ref[...].astype(BF16), w_ref[...])
    d = x_ref.shape[1]
    o_ref[...] = x_ref[...] + zw[:, :d] * jax.nn.sigmoid(zw[:, d:])


def _glu_residual(z, w, x):
    m, d = x.shape
    tm = _row_tile(m, 512)
    return pl.pallas_call(
        _glu_kernel,
        grid=(m // tm,),
        in_specs=[
            pl.BlockSpec((tm, d), lambda i: (i, 0)),
            pl.BlockSpec((d, 2 * d), lambda i: (0, 0)),
            pl.BlockSpec((tm, d), lambda i: (i, 0)),
        ],
        out_specs=pl.BlockSpec((tm, d), lambda i: (i, 0)),
        out_shape=jax.ShapeDtypeStruct((m, d), F32),
        compiler_params=_cparams(("arbitrary",), 48),
        name="glu_residual",
    )(z, w, x)


def _s5_params(a_re, a_im, b_re, b_im, c_re, c_im, log_dt, n_gc):
    groups, n_state = a_re.shape
    gsz = b_re.shape[2]
    a = lax.complex(a_re.astype(F32), a_im.astype(F32))
    dt = jnp.exp(log_dt.astype(F32))[:, None]
    a_bar = jnp.exp(a * dt)
    b_bar = ((a_bar - 1.0) / a)[..., None] * lax.complex(b_re.astype(F32), b_im.astype(F32))
    gpc = groups // n_gc
    eye = jnp.eye(gpc, dtype=F32)

    def in_blockdiag(m):
        m = m.reshape(n_gc, gpc, n_state, gsz).transpose(0, 1, 3, 2)
        return (m[:, :, :, None, :] * eye[None, :, None, :, None]).reshape(n_gc, gpc * gsz, gpc * n_state)

    def out_blockdiag(m):
        m = m.reshape(n_gc, gpc, gsz, n_state).transpose(0, 1, 3, 2)
        return (m[:, :, :, None, :] * eye[None, :, None, :, None]).reshape(n_gc, gpc * n_state, gpc * gsz)

    bbd = jnp.concatenate([in_blockdiag(b_bar.real), in_blockdiag(b_bar.imag)], axis=2).astype(BF16)
    cbd = jnp.concatenate([out_blockdiag(c_re.astype(F32)), -out_blockdiag(c_im.astype(F32))], axis=1).astype(BF16)
    return bbd, cbd, a_bar.real.reshape(1, -1), a_bar.imag.reshape(1, -1)


def kernel(x_prompt, x_sample, mem_prompt, cache_fox_k, cache_fox_v, cache_fox_logf, state_swa_k, state_swa_v, state_ssm_re, state_ssm_im, cache_mem_k, cache_mem_v, page_table, norm_g, final_g, w_ffn_gate, w_ffn_up, w_ffn_down, w_xq, w_xkv, w_xo, w_fox_qkvf, b_fox_f, w_fox_o, w_swa_qkv, swa_sink, w_swa_o, ssm_a_re, ssm_a_im, ssm_b_re, ssm_b_im, ssm_c_re, ssm_c_im, ssm_d, ssm_log_dt, w_ssm_glu):
    batch, seq, d = x_prompt.shape
    n_seq, n_tok, _ = x_sample.shape
    depth = norm_g.shape[0]
    n_mem = mem_prompt.shape[1]
    fox_kvh, fox_hd = cache_fox_k.shape[3], cache_fox_k.shape[4]
    fox_heads = cache_fox_logf.shape[3]
    fox_qd, fox_kd = fox_heads * fox_hd, fox_kvh * fox_hd
    swa_kvh, swa_hd = state_swa_k.shape[3], state_swa_k.shape[4]
    swa_heads = swa_sink.shape[1]
    swa_qd, swa_kd = swa_heads * swa_hd, swa_kvh * swa_hd
    swa_group = swa_heads // swa_kvh
    window = state_swa_k.shape[2]
    x_heads, x_hd = cache_mem_k.shape[3], cache_mem_k.shape[4]
    xd = x_heads * x_hd
    past_len = page_table.shape[1] * cache_fox_k.shape[2]
    n_groups, n_state = ssm_a_re.shape[1], ssm_a_re.shape[2]

    yp = x_prompt.reshape(batch * seq, d)
    ys = x_sample.reshape(n_seq * n_tok, d)
    mem2d = mem_prompt.reshape(batch * n_mem, d)

    wg = w_ffn_gate.astype(BF16)
    wu = w_ffn_up.astype(BF16)
    wd = w_ffn_down.astype(BF16)
    wxq = w_xq.astype(BF16)
    wxkv = w_xkv.astype(BF16)
    wxo = w_xo.astype(BF16)
    wfox = jnp.pad(w_fox_qkvf, ((0, 0), (0, 0), (0, LANES - fox_heads))).astype(BF16)
    bfox = jnp.pad(b_fox_f, ((0, 0), (0, LANES - fox_heads))).astype(F32)
    wfoxo = w_fox_o.astype(BF16)
    wswa = w_swa_qkv.astype(BF16)
    wswao = w_swa_o.astype(BF16)
    wglu = w_ssm_glu.astype(BF16)

    half = swa_hd // 2
    inv_freq = ROPE_THETA ** (-jnp.arange(half, dtype=F32) / half)

    def rope_tables(pos, reps):
        ang = pos.astype(F32)[:, None] * inv_freq[None, :]
        cos = jnp.cos(ang)
        sin = jnp.sin(ang)
        cos_t = jnp.tile(jnp.concatenate([cos, cos], axis=1), (reps, LANES // swa_hd))
        sin_t = jnp.tile(jnp.concatenate([-sin, sin], axis=1), (reps, LANES // swa_hd))
        return cos_t, sin_t

    fox_p, fox_s, swa_p, swa_s, ssm_p, ssm_s, mem_p = [], [], [], [], [], [], []
    for l in range(depth):
        kind, j = l % 3, l // 3
        yp = _ffn(yp, norm_g[l, 0], wg[l, 0], wu[l, 0], wd[l, 0])
        ys = _ffn(ys, norm_g[l, 0], wg[l, 0], wu[l, 0], wd[l, 0])
        if kind == 0:
            qp, kp, vp, lfp, cp = _fox_project(yp, norm_g[l, 1], wfox[j], bfox[j:j + 1], seq,
                                               qd=fox_qd, kd=fox_kd, nh=fox_heads, hd=fox_hd, q_dtype=BF16)
            op = _fox_prompt_attention(qp, kp, vp, cp, batch, seq, fox_heads, fox_kvh, fox_hd)
            yp = _matmul_residual(op, wfoxo[j], yp)
            qs, ks, vs, lfs, _ = _fox_project(ys, norm_g[l, 1], wfox[j], bfox[j:j + 1], n_seq * n_tok,
                                              qd=fox_qd, kd=fox_kd, nh=fox_heads, hd=fox_hd, q_dtype=F32)
            os_ = _fox_sample_attention(qs, ks, vs, lfs, cache_fox_k, cache_fox_v, cache_fox_logf, j,
                                        page_table, n_seq, n_tok, fox_heads, fox_kvh, fox_hd)
            ys = _matmul_residual(os_, wfoxo[j], ys)
            fox_p.append((kp.reshape(batch, seq, fox_kvh, fox_hd), vp.reshape(batch, seq, fox_kvh, fox_hd),
                          lfp.reshape(batch, seq, fox_heads)))
            fox_s.append((ks.reshape(n_seq, n_tok, fox_kvh, fox_hd), vs.reshape(n_seq, n_tok, fox_kvh, fox_hd),
                          lfs.reshape(n_seq, n_tok, fox_heads)))
        elif kind == 1:
            sink_rows = jnp.broadcast_to(swa_sink[j].astype(F32).reshape(1, swa_kvh, swa_group, 1, 1),
                                         (1, swa_kvh, swa_group, window, 1))
            cos_p, sin_p = rope_tables(jnp.arange(seq), batch)
            qp, kp, vp = _swa_project(yp, norm_g[l, 1], wswa[j], cos_p, sin_p,
                                      qd=swa_qd, kd=swa_kd, hd=swa_hd, q_dtype=BF16)
            nb = seq // window
            q_r = _group_rows(qp, batch, nb, window, swa_kvh, swa_group, swa_hd)
            k_r = kp.reshape(batch, seq, swa_kvh, swa_hd).transpose(0, 2, 1, 3)
            v_r = vp.reshape(batch, seq, swa_kvh, swa_hd).transpose(0, 2, 1, 3)
            o_r = _local_attention(q_r, k_r, v_r, k_r, v_r, sink_rows.reshape(1, swa_kvh, swa_group * window, 1),
                                   tq=window, window=window, prev_is_cur=True, out_dtype=BF16)
            op = _ungroup_rows(o_r, batch, nb, window, swa_kvh, swa_group, swa_hd)
            yp = _matmul_residual(op, wswao[j], yp)
            k4 = kp.reshape(batch, seq, swa_kvh, swa_hd)
            v4 = vp.reshape(batch, seq, swa_kvh, swa_hd)
            n_keep = min(window, seq)
            swa_p.append((k4[:, seq - n_keep:], v4[:, seq - n_keep:]))
            cos_s, sin_s = rope_tables(past_len + jnp.arange(n_tok), n_seq)
            qs, ks, vs = _swa_project(ys, norm_g[l, 1], wswa[j], cos_s, sin_s,
                                      qd=swa_qd, kd=swa_kd, hd=swa_hd, q_dtype=F32)
            q_r = _group_rows(qs, n_seq, 1, n_tok, swa_kvh, swa_group, swa_hd)
            kb = state_swa_k[j].transpose(0, 2, 1, 3)
            vb = state_swa_v[j].transpose(0, 2, 1, 3)
            k_r = ks.reshape(n_seq, n_tok, swa_kvh, swa_hd).transpose(0, 2, 1, 3)
            v_r = vs.reshape(n_seq, n_tok, swa_kvh, swa_hd).transpose(0, 2, 1, 3)
            sink_s = sink_rows[:, :, :, :n_tok].reshape(1, swa_kvh, swa_group * n_tok, 1)
            o_r = _local_attention(q_r, kb, vb, k_r, v_r, sink_s, tq=n_tok, window=window, prev_is_cur=False, out_dtype=F32)
            os_ = _ungroup_rows(o_r, n_seq, 1, n_tok, swa_kvh, swa_group, swa_hd)
            ys = _matmul_residual(os_, wswao[j], ys)
            kk = jnp.concatenate([state_swa_k[j], ks.reshape(n_seq, n_tok, swa_kvh, swa_hd)], axis=1)
            vv = jnp.concatenate([state_swa_v[j], vs.reshape(n_seq, n_tok, swa_kvh, swa_hd)], axis=1)
            swa_s.append((kk[:, -window:], vv[:, -window:]))
        else:
            n_gc = 4
            bbd, cbd, a_re, a_im = _s5_params(ssm_a_re[j], ssm_a_im[j], ssm_b_re[j], ssm_b_im[j],
                                              ssm_c_re[j], ssm_c_im[j], ssm_log_dt[j], n_gc)
            ns = n_groups * n_state

            def run(y2d, nbatch, nlen, h0_re, h0_im):
                x_t = y2d.reshape(nbatch, nlen, d).transpose(1, 0, 2)
                z_t, l_re, l_im = _s5_scan(x_t, norm_g[l, 1], bbd, cbd, a_re, a_im, ssm_d[j], h0_re, h0_im)
                out_t = _glu_residual(z_t.reshape(nlen * nbatch, d), wglu[j], x_t.reshape(nlen * nbatch, d))
                y_new = out_t.reshape(nlen, nbatch, d).transpose(1, 0, 2).reshape(nbatch * nlen, d)
                return y_new, l_re.reshape(nbatch, n_groups, n_state), l_im.reshape(nbatch, n_groups, n_state)

            zeros = jnp.zeros((batch, ns), F32)
            yp, rp, ip = run(yp, batch, seq, zeros, zeros)
            ys, rn, inn = run(ys, n_seq, n_tok, state_ssm_re[j].astype(F32).reshape(n_seq, ns),
                              state_ssm_im[j].astype(F32).reshape(n_seq, ns))
            ssm_p.append((rp, ip))
            ssm_s.append((rn, inn))
        kvm = _norm_matmul(mem2d, norm_g[l, 2], wxkv[l], F32)
        mem_p.append((kvm[:, :xd].reshape(batch, n_mem, x_heads, x_hd), kvm[:, xd:].reshape(batch, n_mem, x_heads, x_hd)))
        kvm4 = kvm.reshape(1, batch, n_mem, 2 * xd)
        qx = _norm_matmul(yp, norm_g[l, 3], wxq[l], BF16)
        ox = _cross_attention(qx, kvm4, kvm4, 0, 1, 0, batch, seq, x_heads, x_hd, BF16)
        yp = _matmul_residual(ox, wxo[l], yp)
        qx = _norm_matmul(ys, norm_g[l, 3], wxq[l], F32)
        ox = _cross_attention(qx, cache_mem_k.reshape(depth, n_seq, n_mem * x_heads, x_hd),
                              cache_mem_v.reshape(depth, n_seq, n_mem * x_heads, x_hd),
                              0, 0, l, n_seq, n_tok, x_heads, x_hd, F32, head_rows=True)
        ys = _matmul_residual(ox, wxo[l], ys)
        yp = _ffn(yp, norm_g[l, 4], wg[l, 1], wu[l, 1], wd[l, 1])
        ys = _ffn(ys, norm_g[l, 4], wg[l, 1], wu[l, 1], wd[l, 1])

    y_prompt = _final_norm(yp, final_g).reshape(batch, seq, d)
    y_sample = _final_norm(ys, final_g).reshape(n_seq, n_tok, d)

    def stack(entries, i):
        return jnp.stack([e[i] for e in entries])

    return (y_prompt, y_sample,
            stack(fox_p, 0), stack(fox_p, 1), stack(fox_p, 2),
            stack(swa_p, 0), stack(swa_p, 1),
            stack(ssm_p, 0), stack(ssm_p, 1),
            stack(mem_p, 0), stack(mem_p, 1),
            stack(fox_s, 0), stack(fox_s, 1), stack(fox_s, 2),
            stack(swa_s, 0), stack(swa_s, 1),
            stack(ssm_s, 0), stack(ssm_s, 1))
```

```python
import functools
import math

import jax
import jax.numpy as jnp
from jax import lax
from jax.experimental import pallas as pl
from jax.experimental.pallas import tpu as pltpu

F32 = jnp.float32
BF16 = jnp.bfloat16

RMS_EPS = 1e-6
NEG_INF = -1e30
ROPE_THETA = 10000.0
SSM_GROUP = 16
GELU_C = math.sqrt(2.0 / math.pi)

V7X_VMEM_BYTES = 64 * 1024 * 1024
LANES = 128
SUBLANES = 8


def _cparams(sem, vmem_mb):
    assert vmem_mb * 2**20 < V7X_VMEM_BYTES
    return pltpu.CompilerParams(dimension_semantics=sem, vmem_limit_bytes=vmem_mb * 2**20)


def _rms(x, g):
    ms = jnp.mean(x * x, axis=-1, keepdims=True)
    return (x * lax.rsqrt(ms + RMS_EPS)) * g


def _dot(a, b):
    return jnp.dot(a, b, preferred_element_type=F32)


def _dot_nt(a, b):
    return lax.dot_general(a, b, (((1,), (1,)), ((), ())), preferred_element_type=F32)


def _dot_exact(a, b):
    return jnp.dot(a, b, preferred_element_type=F32, precision=lax.Precision.HIGHEST)


def _row_tile(m, want):
    t = min(want, m)
    assert m % t == 0, (m, t)
    return t


def _ffn_kernel(x_ref, g_ref, wg_ref, wu_ref, wd_ref, o_ref, *, fc):
    x = x_ref[...]
    h = _rms(x, g_ref[...]).astype(BF16)
    acc = None
    for c in range(wg_ref.shape[1] // fc):
        gate = _dot(h, wg_ref[:, c * fc:(c + 1) * fc])
        up = _dot(h, wu_ref[:, c * fc:(c + 1) * fc])
        act = (gate * jax.nn.sigmoid(gate) * up).astype(BF16)
        part = _dot(act, wd_ref[c * fc:(c + 1) * fc, :])
        acc = part if acc is None else acc + part
    o_ref[...] = x + 0.5 * acc


def _ffn(x, g, wg, wu, wd):
    m, d = x.shape
    ff = wg.shape[1]
    tm = _row_tile(m, 1024)
    fc = 256
    assert ff % fc == 0
    once = pl.Buffered(1)
    return pl.pallas_call(
        functools.partial(_ffn_kernel, fc=fc),
        grid=(m // tm,),
        in_specs=[
            pl.BlockSpec((tm, d), lambda i: (i, 0)),
            pl.BlockSpec((1, d), lambda i: (0, 0)),
            pl.BlockSpec((d, ff), lambda i: (0, 0), pipeline_mode=once),
            pl.BlockSpec((d, ff), lambda i: (0, 0), pipeline_mode=once),
            pl.BlockSpec((ff, d), lambda i: (0, 0), pipeline_mode=once),
        ],
        out_specs=pl.BlockSpec((tm, d), lambda i: (i, 0)),
        out_shape=jax.ShapeDtypeStruct((m, d), F32),
        compiler_params=_cparams(("arbitrary",), 56),
        name="ffn",
    )(x, g.reshape(1, d), wg, wu, wd)


def _nmm_kernel(x_ref, g_ref, w_ref, o_ref):
    h = _rms(x_ref[...], g_ref[...]).astype(BF16)
    o_ref[...] = _dot(h, w_ref[...]).astype(o_ref.dtype)


def _norm_matmul(x, g, w, out_dtype):
    m, d = x.shape
    n = w.shape[1]
    tm = _row_tile(m, 512)
    return pl.pallas_call(
        _nmm_kernel,
        grid=(m // tm,),
        in_specs=[
            pl.BlockSpec((tm, d), lambda i: (i, 0)),
            pl.BlockSpec((1, d), lambda i: (0, 0)),
            pl.BlockSpec((d, n), lambda i: (0, 0)),
        ],
        out_specs=pl.BlockSpec((tm, n), lambda i: (i, 0)),
        out_shape=jax.ShapeDtypeStruct((m, n), out_dtype),
        compiler_params=_cparams(("arbitrary",), 40),
        name="norm_matmul",
    )(x, g.reshape(1, d), w)


def _mmres_kernel(a_ref, w_ref, x_ref, o_ref):
    o_ref[...] = x_ref[...] + _dot(a_ref[...].astype(BF16), w_ref[...])


def _matmul_residual(a, w, x):
    m, k = a.shape
    d = w.shape[1]
    tm = _row_tile(m, 512)
    return pl.pallas_call(
        _mmres_kernel,
        grid=(m // tm,),
        in_specs=[
            pl.BlockSpec((tm, k), lambda i: (i, 0)),
            pl.BlockSpec((k, d), lambda i: (0, 0)),
            pl.BlockSpec((tm, d), lambda i: (i, 0)),
        ],
        out_specs=pl.BlockSpec((tm, d), lambda i: (i, 0)),
        out_shape=jax.ShapeDtypeStruct((m, d), F32),
        compiler_params=_cparams(("arbitrary",), 40),
        name="matmul_residual",
    )(a, w, x)


def _final_norm_kernel(x_ref, g_ref, o_ref):
    o_ref[...] = _rms(x_ref[...], g_ref[...])


def _final_norm(x, g):
    m, d = x.shape
    tm = _row_tile(m, 1024)
    return pl.pallas_call(
        _final_norm_kernel,
        grid=(m // tm,),
        in_specs=[pl.BlockSpec((tm, d), lambda i: (i, 0)), pl.BlockSpec((1, d), lambda i: (0, 0))],
        out_specs=pl.BlockSpec((tm, d), lambda i: (i, 0)),
        out_shape=jax.ShapeDtypeStruct((m, d), F32),
        compiler_params=_cparams(("arbitrary",), 40),
        name="final_norm",
    )(x, g.reshape(1, d))


def _fox_proj_kernel(x_ref, g_ref, w_ref, b_ref, q_ref, k_ref, v_ref, lf_ref, c_ref, carry_ref,
                     *, qd, kd, nh, tiles_per_seq, q_scale):
    i = pl.program_id(0)
    h = _rms(x_ref[...], g_ref[...]).astype(BF16)
    proj = _dot(h, w_ref[...])
    q_ref[...] = (proj[:, :qd] * q_scale).astype(q_ref.dtype)
    k_ref[...] = proj[:, qd:qd + kd]
    v_ref[...] = proj[:, qd + kd:qd + 2 * kd]
    z = proj[:, qd + 2 * kd:] + b_ref[...]
    lf = jnp.minimum(z, 0.0) - jnp.log1p(jnp.exp(-jnp.abs(z)))
    lf_ref[...] = lf[:, :nh]

    @pl.when(i % tiles_per_seq == 0)
    def _():
        carry_ref[...] = jnp.zeros_like(carry_ref)

    tm = lf.shape[0]
    r = lax.broadcasted_iota(jnp.int32, (tm, tm), 0)
    c = lax.broadcasted_iota(jnp.int32, (tm, tm), 1)
    tril = jnp.where(c <= r, 1.0, 0.0).astype(F32)
    csum = _dot_exact(tril, lf) + carry_ref[...]
    c_ref[...] = csum[:, :nh]
    carry_ref[...] = csum[tm - 1:tm, :]


def _fox_project(x, g, w_pad, b_pad, seq_len, *, qd, kd, nh, hd, q_dtype):
    m, d = x.shape
    n = w_pad.shape[1]
    tm = _row_tile(seq_len, 512)
    kern = functools.partial(_fox_proj_kernel, qd=qd, kd=kd, nh=nh, tiles_per_seq=seq_len // tm,
                             q_scale=hd ** -0.5)
    return pl.pallas_call(
        kern,
        grid=(m // tm,),
        in_specs=[
            pl.BlockSpec((tm, d), lambda i: (i, 0)),
            pl.BlockSpec((1, d), lambda i: (0, 0)),
            pl.BlockSpec((d, n), lambda i: (0, 0)),
            pl.BlockSpec((1, LANES), lambda i: (0, 0)),
        ],
        out_specs=[
            pl.BlockSpec((tm, qd), lambda i: (i, 0)),
            pl.BlockSpec((tm, kd), lambda i: (i, 0)),
            pl.BlockSpec((tm, kd), lambda i: (i, 0)),
            pl.BlockSpec((tm, nh), lambda i: (i, 0)),
            pl.BlockSpec((tm, nh), lambda i: (i, 0)),
        ],
        out_shape=[
            jax.ShapeDtypeStruct((m, qd), q_dtype),
            jax.ShapeDtypeStruct((m, kd), F32),
            jax.ShapeDtypeStruct((m, kd), F32),
            jax.ShapeDtypeStruct((m, nh), F32),
            jax.ShapeDtypeStruct((m, nh), F32),
        ],
        scratch_shapes=[pltpu.VMEM((1, LANES), F32)],
        compiler_params=_cparams(("arbitrary",), 48),
        name="fox_project",
    )(x, g.reshape(1, d), w_pad, b_pad)


def _split3(x):
    a = lax.reduce_precision(x, exponent_bits=8, mantissa_bits=7)
    r = x - a
    b = lax.reduce_precision(r, exponent_bits=8, mantissa_bits=7)
    return a.astype(BF16), b.astype(BF16), (r - b).astype(BF16)


def _fox_prompt_kernel(q_ref, k_ref, vt_ref, o_ref, *, tq):
    i = pl.program_id(2)
    q = q_ref[0, 0, 0]
    rows = q.shape[0]

    def step(j, carry, masked):
        m, l, acc = carry
        start = pl.multiple_of(j * tq, tq)
        k_j = k_ref[0, 0, pl.ds(start, tq), :]
        vt_j = vt_ref[0, 0, j]
        s = _dot_nt(k_j, q)
        if masked:
            key = lax.broadcasted_iota(jnp.int32, (tq, rows), 0)
            tok = lax.broadcasted_iota(jnp.int32, (tq, rows), 1) % tq
            s = jnp.where(key <= tok, s, NEG_INF)
        m_new = jnp.maximum(m, jnp.max(s, axis=0, keepdims=True))
        alpha = jnp.exp(m - m_new)
        p = jnp.exp(s - m_new)
        l = alpha * l + jnp.sum(p, axis=0, keepdims=True)
        acc = alpha * acc + _dot(vt_j, p.astype(BF16))
        return m_new, l, acc

    hd = vt_ref.shape[3]
    init = (jnp.full((1, rows), NEG_INF, F32), jnp.zeros((1, rows), F32), jnp.zeros((hd, rows), F32))
    carry = lax.fori_loop(0, i, lambda j, cr: step(j, cr, False), init)
    m, l, acc = step(i, carry, True)
    o_ref[0, 0, 0] = (acc / l).astype(o_ref.dtype)


def _fox_prompt_attention(q, k, v, c, batch, seq_len, n_heads, kv_heads, hd):
    group = n_heads // kv_heads
    tq = _row_tile(seq_len, 256)
    nq = seq_len // tq
    nterm = 3
    assert hd + 2 * nterm * group <= LANES
    c1, c2, c3 = _split3(c.reshape(batch, seq_len, kv_heads, group))
    one = jnp.ones_like(c1)
    eye = jnp.eye(group, dtype=BF16)
    q_bias = jnp.stack([c1, c2, c3, one, one, one], axis=-1)
    q_bias = (q_bias[:, :, :, :, None, :] * eye[None, None, None, :, :, None]).reshape(
        batch, seq_len, kv_heads, group, 2 * nterm * group)
    k_bias = jnp.stack([one, one, one, -c1, -c2, -c3], axis=-1).reshape(
        batch, seq_len, kv_heads, 2 * nterm * group)
    pad = LANES - hd - 2 * nterm * group
    q_aug = jnp.concatenate([q.reshape(batch, seq_len, kv_heads, group, hd), q_bias,
                             jnp.zeros((batch, seq_len, kv_heads, group, pad), BF16)], axis=-1)
    k_aug = jnp.concatenate([k.astype(BF16).reshape(batch, seq_len, kv_heads, hd), k_bias,
                             jnp.zeros((batch, seq_len, kv_heads, pad), BF16)], axis=-1)
    q_r = (q_aug.reshape(batch, nq, tq, kv_heads, group, LANES).transpose(0, 3, 1, 4, 2, 5)
           .reshape(batch, kv_heads, nq, group * tq, LANES))
    k_r = k_aug.transpose(0, 2, 1, 3)
    vt_r = v.astype(BF16).reshape(batch, nq, tq, kv_heads, hd).transpose(0, 3, 1, 4, 2)
    kern = functools.partial(_fox_prompt_kernel, tq=tq)
    o = pl.pallas_call(
        kern,
        grid=(batch, kv_heads, nq),
        in_specs=[
            pl.BlockSpec((1, 1, 1, group * tq, LANES), lambda b, kv, i: (b, kv, i, 0, 0)),
            pl.BlockSpec((1, 1, seq_len, LANES), lambda b, kv, i: (b, kv, 0, 0)),
            pl.BlockSpec((1, 1, nq, hd, tq), lambda b, kv, i: (b, kv, 0, 0, 0)),
        ],
        out_specs=pl.BlockSpec((1, 1, 1, hd, group * tq), lambda b, kv, i: (b, kv, i, 0, 0)),
        out_shape=jax.ShapeDtypeStruct((batch, kv_heads, nq, hd, group * tq), BF16),
        compiler_params=_cparams(("arbitrary", "arbitrary", "arbitrary"), 40),
        name="fox_prompt_attention",
    )(q_r, k_r, vt_r)
    return (o.reshape(batch, kv_heads, nq, hd, group, tq).transpose(0, 2, 5, 1, 4, 3)
            .reshape(batch * seq_len, n_heads * hd))


def _fox_sample_kernel(pt_ref, q_ref, knt_ref, vnt_ref, lfr_ref, kpool, vpool, fpool, o_ref,
                       kbuf, vbuf, fbuf, sem, *, layer, n_seq, cpp, n_chunks, page, n_tok, heads, hd, kv_heads):
    b = pl.program_id(0)
    rows = heads * n_tok
    ckeys = cpp * page

    def copies(seq, chunk, slot):
        out = []
        for pi in range(cpp):
            pg = pt_ref[seq, chunk * cpp + pi]
            lanes_pi = pl.ds(pi * page, page)
            out.append(pltpu.make_async_copy(kpool.at[layer, pg], kbuf.at[slot, :, lanes_pi], sem.at[slot]))
            out.append(pltpu.make_async_copy(vpool.at[layer, pg], vbuf.at[slot, :, lanes_pi], sem.at[slot]))
            out.append(pltpu.make_async_copy(fpool.at[layer, pg], fbuf.at[slot, :, lanes_pi], sem.at[slot]))
        return out

    n_slots = kbuf.shape[0]
    total = n_seq * n_chunks

    def chunk_copies(g):
        seq = g // n_chunks
        return copies(seq, n_chunks - 1 - (g - seq * n_chunks), g % n_slots)

    def start(g):
        for cp in chunk_copies(g):
            cp.start()

    def wait(g):
        for cp in chunk_copies(g):
            cp.wait()

    @pl.when(b == 0)
    def _():
        for g0 in range(min(n_slots - 1, total)):
            start(g0)

    q = q_ref[0]
    lfr = lfr_ref[0]
    lane = lax.broadcasted_iota(jnp.int32, (rows, page), 1)
    trow = lax.broadcasted_iota(jnp.int32, (rows, page), 0) % n_tok
    ckn = lfr
    k = 1
    while k < n_tok:
        ckn = ckn + jnp.where(lane >= k, pltpu.roll(ckn, k, 1), 0.0)
        k *= 2
    cq = jnp.sum(jnp.where(lane == trow, ckn, 0.0), axis=1, keepdims=True)

    s = _dot(q, knt_ref[0].astype(BF16))
    s = jnp.where(lane <= trow, s + (cq - ckn), NEG_INF)
    m0 = jnp.max(s, axis=1, keepdims=True)
    p = jnp.exp(s - m0)
    l0 = jnp.sum(p, axis=1, keepdims=True)
    acc0 = _dot_nt(p.astype(BF16), vnt_ref[0].astype(BF16))

    lane_c = lax.broadcasted_iota(jnp.int32, (heads, ckeys), 1)

    def chunk_body(ci, carry):
        m, l, acc, tail = carry
        g = b * n_chunks + ci
        slot = g % n_slots
        wait(g)

        @pl.when(g + (n_slots - 1) < total)
        def _():
            start(g + (n_slots - 1))

        f = fbuf[slot]
        r = f
        k = 1
        while k < ckeys:
            r = r + jnp.where(lane_c < ckeys - k, pltpu.roll(r, ckeys - k, 1), 0.0)
            k *= 2
        suf = (r - f) + tail
        bias = jnp.concatenate(
            [jnp.broadcast_to(suf[hh:hh + 1, :], (n_tok, ckeys)) for hh in range(heads)], axis=0)
        s = _dot(q, kbuf[slot].astype(BF16)) + (bias + cq)
        m_new = jnp.maximum(m, jnp.max(s, axis=1, keepdims=True))
        alpha = jnp.exp(m - m_new)
        p = jnp.exp(s - m_new)
        l = alpha * l + jnp.sum(p, axis=1, keepdims=True)
        acc = alpha * acc + _dot_nt(p.astype(BF16), vbuf[slot].astype(BF16))
        return m_new, l, acc, tail + r[:, 0:1]

    init = (m0, l0, acc0, jnp.zeros((heads, 1), F32))
    m, l, acc, _ = lax.fori_loop(0, n_chunks, chunk_body, init)
    o = acc / l
    rpk = rows // kv_heads
    o_ref[0] = jnp.concatenate([o[kk * rpk:(kk + 1) * rpk, kk * hd:(kk + 1) * hd] for kk in range(kv_heads)],
                               axis=0)


def _fox_sample_attention(q, k_new, v_new, logf_new, k_pools, v_pools, f_pools, layer, page_table,
                          n_seq, n_tok, n_heads, kv_heads, hd):
    group = n_heads // kv_heads
    n_layers, n_pool, page = k_pools.shape[0], k_pools.shape[1], k_pools.shape[2]
    n_pages = page_table.shape[1]
    width = kv_heads * hd
    rows = n_heads * n_tok
    cpp = 8 if n_pages % 8 == 0 else 1
    assert page % LANES == 0 and n_tok <= page
    n_chunks = n_pages // cpp
    q_t = q.reshape(n_seq, n_tok, kv_heads, group, hd).transpose(0, 2, 3, 1, 4).reshape(n_seq, kv_heads, group * n_tok, hd)
    eye = jnp.eye(kv_heads, dtype=F32)
    q_bd = (q_t[:, :, :, None, :] * eye[None, :, None, :, None]).reshape(n_seq, rows, width).astype(BF16)
    pad = ((0, 0), (0, 0), (0, page - n_tok))
    lfr = jnp.broadcast_to(logf_new.reshape(n_seq, n_tok, n_heads).transpose(0, 2, 1)[:, :, None, :],
                           (n_seq, n_heads, n_tok, n_tok)).reshape(n_seq, rows, n_tok)
    lfr = jnp.pad(lfr, pad)
    knt = jnp.pad(k_new.reshape(n_seq, n_tok, width).transpose(0, 2, 1), pad)
    vnt = jnp.pad(v_new.reshape(n_seq, n_tok, width).transpose(0, 2, 1), pad)
    kp = k_pools.transpose(0, 1, 3, 4, 2).reshape(n_layers, n_pool, width, page)
    vp = v_pools.transpose(0, 1, 3, 4, 2).reshape(n_layers, n_pool, width, page)
    fp = f_pools.transpose(0, 1, 3, 2)
    n_slots = 4
    kern = functools.partial(_fox_sample_kernel, layer=layer, n_seq=n_seq, cpp=cpp, n_chunks=n_chunks, page=page,
                             n_tok=n_tok, heads=n_heads, hd=hd, kv_heads=kv_heads)
    grid_spec = pltpu.PrefetchScalarGridSpec(
        num_scalar_prefetch=1,
        grid=(n_seq,),
        in_specs=[
            pl.BlockSpec((1, rows, width), lambda b, pt: (b, 0, 0)),
            pl.BlockSpec((1, width, page), lambda b, pt: (b, 0, 0)),
            pl.BlockSpec((1, width, page), lambda b, pt: (b, 0, 0)),
            pl.BlockSpec((1, rows, page), lambda b, pt: (b, 0, 0)),
            pl.BlockSpec(memory_space=pl.ANY),
            pl.BlockSpec(memory_space=pl.ANY),
            pl.BlockSpec(memory_space=pl.ANY),
        ],
        out_specs=pl.BlockSpec((1, rows, hd), lambda b, pt: (b, 0, 0)),
        scratch_shapes=[
            pltpu.VMEM((n_slots, width, cpp * page), F32),
            pltpu.VMEM((n_slots, width, cpp * page), F32),
            pltpu.VMEM((n_slots, n_heads, cpp * page), F32),
            pltpu.SemaphoreType.DMA((n_slots,)),
        ],
    )
    o = pl.pallas_call(
        kern,
        grid_spec=grid_spec,
        out_shape=jax.ShapeDtypeStruct((n_seq, rows, hd), F32),
        compiler_params=_cparams(("arbitrary",), 40),
        name="fox_sample_attention",
    )(page_table, q_bd, knt, vnt, lfr, kp, vp, fp)
    return (o.reshape(n_seq, kv_heads, group, n_tok, hd).transpose(0, 3, 1, 2, 4)
            .reshape(n_seq * n_tok, n_heads * hd))


def _swa_proj_kernel(x_ref, g_ref, w_ref, cos_ref, sin_ref, q_ref, k_ref, v_ref, *, qd, kd, hd, q_scale):
    h = _rms(x_ref[...], g_ref[...]).astype(BF16)
    proj = _dot(h, w_ref[...])
    cos = cos_ref[...]
    sin = sin_ref[...]
    half = hd // 2

    def rope(x):
        n = x.shape[1]
        reps = n // LANES
        cs = jnp.concatenate([cos] * reps, axis=1) if reps > 1 else cos
        sn = jnp.concatenate([sin] * reps, axis=1) if reps > 1 else sin
        lane = lax.broadcasted_iota(jnp.int32, x.shape, 1) % hd
        up = pltpu.roll(x, n - half, 1)
        dn = pltpu.roll(x, half, 1)
        return x * cs + jnp.where(lane < half, up, dn) * sn

    q_ref[...] = (rope(proj[:, :qd]) * q_scale).astype(q_ref.dtype)
    k_ref[...] = rope(proj[:, qd:qd + kd])
    v_ref[...] = proj[:, qd + kd:qd + 2 * kd]


def _swa_project(x, g, w, cos, sin, *, qd, kd, hd, q_dtype):
    m, d = x.shape
    n = w.shape[1]
    tm = _row_tile(m, 512)
    kern = functools.partial(_swa_proj_kernel, qd=qd, kd=kd, hd=hd, q_scale=hd ** -0.5)
    return pl.pallas_call(
        kern,
        grid=(m // tm,),
        in_specs=[
            pl.BlockSpec((tm, d), lambda i: (i, 0)),
            pl.BlockSpec((1, d), lambda i: (0, 0)),
            pl.BlockSpec((d, n), lambda i: (0, 0)),
            pl.BlockSpec((tm, LANES), lambda i: (i, 0)),
            pl.BlockSpec((tm, LANES), lambda i: (i, 0)),
        ],
        out_specs=[
            pl.BlockSpec((tm, qd), lambda i: (i, 0)),
            pl.BlockSpec((tm, kd), lambda i: (i, 0)),
            pl.BlockSpec((tm, kd), lambda i: (i, 0)),
        ],
        out_shape=[
            jax.ShapeDtypeStruct((m, qd), q_dtype),
            jax.ShapeDtypeStruct((m, kd), F32),
            jax.ShapeDtypeStruct((m, kd), F32),
        ],
        compiler_params=_cparams(("arbitrary",), 48),
        name="swa_project",
    )(x, g.reshape(1, d), w, cos, sin)


def _local_attn_kernel(q_ref, kp_ref, vp_ref, kc_ref, vc_ref, sink_ref, o_ref, *, tq, window, first_block_has_prev):
    i = pl.program_id(2)
    q = q_ref[0, 0, 0].astype(BF16)
    rows = q.shape[0]
    kp = kp_ref[0, 0].astype(BF16)
    vp = vp_ref[0, 0].astype(BF16)
    kc = kc_ref[0, 0].astype(BF16)
    vc = vc_ref[0, 0].astype(BF16)
    w = kp.shape[0]
    t_p = lax.broadcasted_iota(jnp.int32, (rows, w), 0) % tq
    j_p = lax.broadcasted_iota(jnp.int32, (rows, w), 1)
    ok_p = (w + t_p - j_p) <= window
    if not first_block_has_prev:
        ok_p = jnp.logical_and(ok_p, i > 0)
    t_c = lax.broadcasted_iota(jnp.int32, (rows, tq), 0) % tq
    j_c = lax.broadcasted_iota(jnp.int32, (rows, tq), 1)
    ok_c = j_c <= t_c
    s_p = jnp.where(ok_p, _dot_nt(q, kp), NEG_INF)
    s_c = jnp.where(ok_c, _dot_nt(q, kc), NEG_INF)
    sink = sink_ref[0, 0]
    m = jnp.maximum(jnp.maximum(jnp.max(s_p, axis=1, keepdims=True), jnp.max(s_c, axis=1, keepdims=True)), sink)
    p_p = jnp.exp(s_p - m)
    p_c = jnp.exp(s_c - m)
    l = jnp.sum(p_p, axis=1, keepdims=True) + jnp.sum(p_c, axis=1, keepdims=True) + jnp.exp(sink - m)
    acc = _dot(p_p.astype(BF16), vp) + _dot(p_c.astype(BF16), vc)
    o_ref[0, 0, 0] = (acc / l).astype(o_ref.dtype)


def _local_attention(q_r, k_prev, v_prev, k_cur, v_cur, sink_rows, *, tq, window, prev_is_cur, out_dtype):
    b, kvh, nb, rows, hd = q_r.shape
    w = tq if prev_is_cur else k_prev.shape[2]
    if prev_is_cur:
        prev_map = lambda bb, kv, i: (bb, kv, jnp.maximum(i - 1, 0), 0)
    else:
        assert nb == 1
        prev_map = lambda bb, kv, i: (bb, kv, 0, 0)
    kern = functools.partial(_local_attn_kernel, tq=tq, window=window, first_block_has_prev=not prev_is_cur)
    return pl.pallas_call(
        kern,
        grid=(b, kvh, nb),
        in_specs=[
            pl.BlockSpec((1, 1, 1, rows, hd), lambda bb, kv, i: (bb, kv, i, 0, 0)),
            pl.BlockSpec((1, 1, w, hd), prev_map),
            pl.BlockSpec((1, 1, w, hd), prev_map),
            pl.BlockSpec((1, 1, tq, hd), lambda bb, kv, i: (bb, kv, i, 0)),
            pl.BlockSpec((1, 1, tq, hd), lambda bb, kv, i: (bb, kv, i, 0)),
            pl.BlockSpec((1, 1, rows, 1), lambda bb, kv, i: (0, kv, 0, 0)),
        ],
        out_specs=pl.BlockSpec((1, 1, 1, rows, hd), lambda bb, kv, i: (bb, kv, i, 0, 0)),
        out_shape=jax.ShapeDtypeStruct((b, kvh, nb, rows, hd), out_dtype),
        compiler_params=_cparams(("arbitrary", "arbitrary", "arbitrary"), 40),
        name="local_attention",
    )(q_r, k_prev, v_prev, k_cur, v_cur, sink_rows)


def _group_rows(q, batch, nb, tq, kv_heads, group, hd):
    return (q.reshape(batch, nb, tq, kv_heads, group, hd).transpose(0, 3, 1, 4, 2, 5)
            .reshape(batch, kv_heads, nb, group * tq, hd))


def _ungroup_rows(o, batch, nb, tq, kv_heads, group, hd):
    return (o.reshape(batch, kv_heads, nb, group, tq, hd).transpose(0, 2, 4, 1, 3, 5)
            .reshape(batch * nb * tq, kv_heads * group * hd))


def _cross_attn_kernel(q_ref, k_ref, v_ref, o_ref, *, heads, hd, scale, head_rows):
    outs = []
    for h in range(heads):
        q = q_ref[0, :, h * hd:(h + 1) * hd].astype(BF16)
        if head_rows:
            n_mem = k_ref.shape[2] // heads
            k = k_ref[0, 0, pl.ds(h, n_mem, stride=heads), :].astype(BF16)
            v = v_ref[0, 0, pl.ds(h, n_mem, stride=heads), :].astype(BF16)
        else:
            k = k_ref[0, 0, :, h * hd:(h + 1) * hd].astype(BF16)
            v = v_ref[0, 0, :, h * hd:(h + 1) * hd].astype(BF16)
        s = _dot_nt(q, k) * scale
        m = jnp.max(s, axis=1, keepdims=True)
        p = jnp.exp(s - m)
        l = jnp.sum(p, axis=1, keepdims=True)
        outs.append(_dot(p.astype(BF16), v) / l)
    o_ref[0] = jnp.concatenate(outs, axis=1).astype(o_ref.dtype)


def _cross_attention(q, mk, mv, k_col, v_col, layer, batch, seq_len, heads, hd, out_dtype, head_rows=False):
    width = heads * hd
    tq = _row_tile(seq_len, 512)
    nq = seq_len // tq
    kv_block = (1, 1, mk.shape[2], hd if head_rows else width)
    kern = functools.partial(_cross_attn_kernel, heads=heads, hd=hd, scale=hd ** -0.5, head_rows=head_rows)
    o = pl.pallas_call(
        kern,
        grid=(batch, nq),
        in_specs=[
            pl.BlockSpec((1, tq, width), lambda b, i: (b, i, 0)),
            pl.BlockSpec(kv_block, lambda b, i: (layer, b, 0, k_col)),
            pl.BlockSpec(kv_block, lambda b, i: (layer, b, 0, v_col)),
        ],
        out_specs=pl.BlockSpec((1, tq, width), lambda b, i: (b, i, 0)),
        out_shape=jax.ShapeDtypeStruct((batch, seq_len, width), out_dtype),
        compiler_params=_cparams(("arbitrary", "arbitrary"), 40),
        name="cross_attention",
    )(q.reshape(batch, seq_len, width), mk, mv)
    return o.reshape(batch * seq_len, width)


def _s5_kernel(x_ref, g_ref, bbd_ref, cbd_ref, are_ref, aim_ref, d_ref, h0re_ref, h0im_ref,
               z_ref, lre_ref, lim_ref, st_ref, sre_ref, sim_ref, *, tl, n_gc, gc_in, gc_st, cw):
    li = pl.program_id(1)

    @pl.when(li == 0)
    def _():
        sre_ref[...] = h0re_ref[...]
        sim_ref[...] = h0im_ref[...]

    d_model = x_ref.shape[2]
    rows = tl * SUBLANES
    x = x_ref[...].reshape(rows, d_model)
    h = _rms(x, g_ref[...])
    hb = h.astype(BF16)
    for k in range(n_gc):
        st_ref[:, k * 2 * gc_st:(k + 1) * 2 * gc_st] = _dot(hb[:, k * gc_in:(k + 1) * gc_in], bbd_ref[k])

    for k in range(n_gc):
        for c in range(gc_st // cw):
            re0 = k * 2 * gc_st + c * cw
            im0 = re0 + gc_st
            sc = k * gc_st + c * cw
            ar = jnp.broadcast_to(are_ref[:, sc:sc + cw], (SUBLANES, cw))
            ai = jnp.broadcast_to(aim_ref[:, sc:sc + cw], (SUBLANES, cw))

            def body(l, carry, re0=re0, im0=im0, ar=ar, ai=ai):
                sr, si = carry
                r0 = pl.multiple_of(l * SUBLANES, SUBLANES)
                br = st_ref[pl.ds(r0, SUBLANES), re0:re0 + cw]
                bi = st_ref[pl.ds(r0, SUBLANES), im0:im0 + cw]
                nr = ar * sr - ai * si + br
                ni = ar * si + ai * sr + bi
                st_ref[pl.ds(r0, SUBLANES), re0:re0 + cw] = nr
                st_ref[pl.ds(r0, SUBLANES), im0:im0 + cw] = ni
                return nr, ni

            sr, si = lax.fori_loop(0, tl, body, (sre_ref[:, sc:sc + cw], sim_ref[:, sc:sc + cw]))
            sre_ref[:, sc:sc + cw] = sr
            sim_ref[:, sc:sc + cw] = si

    ys = [_dot(st_ref[:, k * 2 * gc_st:(k + 1) * 2 * gc_st].astype(BF16), cbd_ref[k]) for k in range(n_gc)]
    y = jnp.concatenate(ys, axis=1) + d_ref[...] * h
    z = y * (0.5 * (1.0 + jnp.tanh(GELU_C * (y + 0.044715 * (y * y * y)))))
    z_ref[...] = z.reshape(tl, SUBLANES, d_model)

    @pl.when(li == pl.num_programs(1) - 1)
    def _():
        lre_ref[...] = sre_ref[...]
        lim_ref[...] = sim_ref[...]


def _s5_scan(x_t, g, bbd, cbd, a_re, a_im, d_skip, h0_re, h0_im):
    seq_len, bt, d = x_t.shape
    n_gc, gc_in, two_gc_st = bbd.shape
    gc_st = two_gc_st // 2
    ns = n_gc * gc_st
    tl = _row_tile(seq_len, 32)
    kern = functools.partial(_s5_kernel, tl=tl, n_gc=n_gc, gc_in=gc_in, gc_st=gc_st, cw=512)
    return pl.pallas_call(
        kern,
        grid=(bt // SUBLANES, seq_len // tl),
        in_specs=[
            pl.BlockSpec((tl, SUBLANES, d), lambda b, l: (l, b, 0)),
            pl.BlockSpec((1, d), lambda b, l: (0, 0)),
            pl.BlockSpec((n_gc, gc_in, 2 * gc_st), lambda b, l: (0, 0, 0)),
            pl.BlockSpec((n_gc, 2 * gc_st, gc_in), lambda b, l: (0, 0, 0)),
            pl.BlockSpec((1, ns), lambda b, l: (0, 0)),
            pl.BlockSpec((1, ns), lambda b, l: (0, 0)),
            pl.BlockSpec((1, d), lambda b, l: (0, 0)),
            pl.BlockSpec((SUBLANES, ns), lambda b, l: (b, 0)),
            pl.BlockSpec((SUBLANES, ns), lambda b, l: (b, 0)),
        ],
        out_specs=[
            pl.BlockSpec((tl, SUBLANES, d), lambda b, l: (l, b, 0)),
            pl.BlockSpec((SUBLANES, ns), lambda b, l: (b, 0)),
            pl.BlockSpec((SUBLANES, ns), lambda b, l: (b, 0)),
        ],
        out_shape=[
            jax.ShapeDtypeStruct((seq_len, bt, d), F32),
            jax.ShapeDtypeStruct((bt, ns), F32),
            jax.ShapeDtypeStruct((bt, ns), F32),
        ],
        scratch_shapes=[
            pltpu.VMEM((tl * SUBLANES, 2 * ns), F32),
            pltpu.VMEM((SUBLANES, ns), F32),
            pltpu.VMEM((SUBLANES, ns), F32),
        ],
        compiler_params=_cparams(("arbitrary", "arbitrary"), 48),
        name="s5_scan",
    )(x_t, g.reshape(1, d), bbd, cbd, a_re, a_im, d_skip.reshape(1, d), h0_re, h0_im)


def _glu_kernel(z_ref, w_ref, x_ref, o_ref):
    zw = _dot(z_ref[...].astype(BF16), w_ref[...])
    d = x_ref.shape[1]
    o_ref[...] = x_ref[...] + zw[:, :d] * jax.nn.sigmoid(zw[:, d:])


def _glu_residual(z, w, x):
    m, d = x.shape
    tm = _row_tile(m, 512)
    return pl.pallas_call(
        _glu_kernel,
        grid=(m // tm,),
        in_specs=[
            pl.BlockSpec((tm, d), lambda i: (i, 0)),
            pl.BlockSpec((d, 2 * d), lambda i: (0, 0)),
            pl.BlockSpec((tm, d), lambda i: (i, 0)),
        ],
        out_specs=pl.BlockSpec((tm, d), lambda i: (i, 0)),
        out_shape=jax.ShapeDtypeStruct((m, d), F32),
        compiler_params=_cparams(("arbitrary",), 48),
        name="glu_residual",
    )(z, w, x)


def _s5_params(a_re, a_im, b_re, b_im, c_re, c_im, log_dt, n_gc):
    groups, n_state = a_re.shape
    gsz = b_re.shape[2]
    a = lax.complex(a_re.astype(F32), a_im.astype(F32))
    dt = jnp.exp(log_dt.astype(F32))[:, None]
    a_bar = jnp.exp(a * dt)
    b_bar = ((a_bar - 1.0) / a)[..., None] * lax.complex(b_re.astype(F32), b_im.astype(F32))
    gpc = groups // n_gc
    eye = jnp.eye(gpc, dtype=F32)

    def in_blockdiag(m):
        m = m.reshape(n_gc, gpc, n_state, gsz).transpose(0, 1, 3, 2)
        return (m[:, :, :, None, :] * eye[None, :, None, :, None]).reshape(n_gc, gpc * gsz, gpc * n_state)

    def out_blockdiag(m):
        m = m.reshape(n_gc, gpc, gsz, n_state).transpose(0, 1, 3, 2)
        return (m[:, :, :, None, :] * eye[None, :, None, :, None]).reshape(n_gc, gpc * n_state, gpc * gsz)

    bbd = jnp.concatenate([in_blockdiag(b_bar.real), in_blockdiag(b_bar.imag)], axis=2).astype(BF16)
    cbd = jnp.concatenate([out_blockdiag(c_re.astype(F32)), -out_blockdiag(c_im.astype(F32))], axis=1).astype(BF16)
    return bbd, cbd, a_bar.real.reshape(1, -1), a_bar.imag.reshape(1, -1)


def kernel(x_prompt, x_sample, mem_prompt, cache_fox_k, cache_fox_v, cache_fox_logf, state_swa_k, state_swa_v, state_ssm_re, state_ssm_im, cache_mem_k, cache_mem_v, page_table, norm_g, final_g, w_ffn_gate, w_ffn_up, w_ffn_down, w_xq, w_xkv, w_xo, w_fox_qkvf, b_fox_f, w_fox_o, w_swa_qkv, swa_sink, w_swa_o, ssm_a_re, ssm_a_im, ssm_b_re, ssm_b_im, ssm_c_re, ssm_c_im, ssm_d, ssm_log_dt, w_ssm_glu):
    batch, seq, d = x_prompt.shape
    n_seq, n_tok, _ = x_sample.shape
    depth = norm_g.shape[0]
    n_mem = mem_prompt.shape[1]
    fox_kvh, fox_hd = cache_fox_k.shape[3], cache_fox_k.shape[4]
    fox_heads = cache_fox_logf.shape[3]
    fox_qd, fox_kd = fox_heads * fox_hd, fox_kvh * fox_hd
    swa_kvh, swa_hd = state_swa_k.shape[3], state_swa_k.shape[4]
    swa_heads = swa_sink.shape[1]
    swa_qd, swa_kd = swa_heads * swa_hd, swa_kvh * swa_hd
    swa_group = swa_heads // swa_kvh
    window = state_swa_k.shape[2]
    x_heads, x_hd = cache_mem_k.shape[3], cache_mem_k.shape[4]
    xd = x_heads * x_hd
    past_len = page_table.shape[1] * cache_fox_k.shape[2]
    n_groups, n_state = ssm_a_re.shape[1], ssm_a_re.shape[2]

    yp = x_prompt.reshape(batch * seq, d)
    ys = x_sample.reshape(n_seq * n_tok, d)
    mem2d = mem_prompt.reshape(batch * n_mem, d)

    wg = w_ffn_gate.astype(BF16)
    wu = w_ffn_up.astype(BF16)
    wd = w_ffn_down.astype(BF16)
    wxq = w_xq.astype(BF16)
    wxkv = w_xkv.astype(BF16)
    wxo = w_xo.astype(BF16)
    wfox = jnp.pad(w_fox_qkvf, ((0, 0), (0, 0), (0, LANES - fox_heads))).astype(BF16)
    bfox = jnp.pad(b_fox_f, ((0, 0), (0, LANES - fox_heads))).astype(F32)
    wfoxo = w_fox_o.astype(BF16)
    wswa = w_swa_qkv.astype(BF16)
    wswao = w_swa_o.astype(BF16)
    wglu = w_ssm_glu.astype(BF16)

    half = swa_hd // 2
    inv_freq = ROPE_THETA ** (-jnp.arange(half, dtype=F32) / half)

    def rope_tables(pos, reps):
        ang = pos.astype(F32)[:, None] * inv_freq[None, :]
        cos = jnp.cos(ang)
        sin = jnp.sin(ang)
        cos_t = jnp.tile(jnp.concatenate([cos, cos], axis=1), (reps, LANES // swa_hd))
        sin_t = jnp.tile(jnp.concatenate([-sin, sin], axis=1), (reps, LANES // swa_hd))
        return cos_t, sin_t

    fox_p, fox_s, swa_p, swa_s, ssm_p, ssm_s, mem_p = [], [], [], [], [], [], []
    for l in range(depth):
        kind, j = l % 3, l // 3
        yp = _ffn(yp, norm_g[l, 0], wg[l, 0], wu[l, 0], wd[l, 0])
        ys = _ffn(ys, norm_g[l, 0], wg[l, 0], wu[l, 0], wd[l, 0])
        if kind == 0:
            qp, kp, vp, lfp, cp = _fox_project(yp, norm_g[l, 1], wfox[j], bfox[j:j + 1], seq,
                                               qd=fox_qd, kd=fox_kd, nh=fox_heads, hd=fox_hd, q_dtype=BF16)
            op = _fox_prompt_attention(qp, kp, vp, cp, batch, seq, fox_heads, fox_kvh, fox_hd)
            yp = _matmul_residual(op, wfoxo[j], yp)
            qs, ks, vs, lfs, _ = _fox_project(ys, norm_g[l, 1], wfox[j], bfox[j:j + 1], n_seq * n_tok,
                                              qd=fox_qd, kd=fox_kd, nh=fox_heads, hd=fox_hd, q_dtype=F32)
            os_ = _fox_sample_attention(qs, ks, vs, lfs, cache_fox_k, cache_fox_v, cache_fox_logf, j,
                                        page_table, n_seq, n_tok, fox_heads, fox_kvh, fox_hd)
            ys = _matmul_residual(os_, wfoxo[j], ys)
            fox_p.append((kp.reshape(batch, seq, fox_kvh, fox_hd), vp.reshape(batch, seq, fox_kvh, fox_hd),
                          lfp.reshape(batch, seq, fox_heads)))
            fox_s.append((ks.reshape(n_seq, n_tok, fox_kvh, fox_hd), vs.reshape(n_seq, n_tok, fox_kvh, fox_hd),
                          lfs.reshape(n_seq, n_tok, fox_heads)))
        elif kind == 1:
            sink_rows = jnp.broadcast_to(swa_sink[j].astype(F32).reshape(1, swa_kvh, swa_group, 1, 1),
                                         (1, swa_kvh, swa_group, window, 1))
            cos_p, sin_p = rope_tables(jnp.arange(seq), batch)
            qp, kp, vp = _swa_project(yp, norm_g[l, 1], wswa[j], cos_p, sin_p,
                                      qd=swa_qd, kd=swa_kd, hd=swa_hd, q_dtype=BF16)
            nb = seq // window
            q_r = _group_rows(qp, batch, nb, window, swa_kvh, swa_group, swa_hd)
            k_r = kp.reshape(batch, seq, swa_kvh, swa_hd).transpose(0, 2, 1, 3)
            v_r = vp.reshape(batch, seq, swa_kvh, swa_hd).transpose(0, 2, 1, 3)
            o_r = _local_attention(q_r, k_r, v_r, k_r, v_r, sink_rows.reshape(1, swa_kvh, swa_group * window, 1),
                                   tq=window, window=window, prev_is_cur=True, out_dtype=BF16)
            op = _ungroup_rows(o_r, batch, nb, window, swa_kvh, swa_group, swa_hd)
            yp = _matmul_residual(op, wswao[j], yp)
            k4 = kp.reshape(batch, seq, swa_kvh, swa_hd)
            v4 = vp.reshape(batch, seq, swa_kvh, swa_hd)
            n_keep = min(window, seq)
            swa_p.append((k4[:, seq - n_keep:], v4[:, seq - n_keep:]))
            cos_s, sin_s = rope_tables(past_len + jnp.arange(n_tok), n_seq)
            qs, ks, vs = _swa_project(ys, norm_g[l, 1], wswa[j], cos_s, sin_s,
                                      qd=swa_qd, kd=swa_kd, hd=swa_hd, q_dtype=F32)
            q_r = _group_rows(qs, n_seq, 1, n_tok, swa_kvh, swa_group, swa_hd)
            kb = state_swa_k[j].transpose(0, 2, 1, 3)
            vb = state_swa_v[j].transpose(0, 2, 1, 3)
            k_r = ks.reshape(n_seq, n_tok, swa_kvh, swa_hd).transpose(0, 2, 1, 3)
            v_r = vs.reshape(n_seq, n_tok, swa_kvh, swa_hd).transpose(0, 2, 1, 3)
            sink_s = sink_rows[:, :, :, :n_tok].reshape(1, swa_kvh, swa_group * n_tok, 1)
            o_r = _local_attention(q_r, kb, vb, k_r, v_r, sink_s, tq=n_tok, window=window, prev_is_cur=False, out_dtype=F32)
            os_ = _ungroup_rows(o_r, n_seq, 1, n_tok, swa_kvh, swa_group, swa_hd)
            ys = _matmul_residual(os_, wswao[j], ys)
            kk = jnp.concatenate([state_swa_k[j], ks.reshape(n_seq, n_tok, swa_kvh, swa_hd)], axis=1)
            vv = jnp.concatenate([state_swa_v[j], vs.reshape(n_seq, n_tok, swa_kvh, swa_hd)], axis=1)
            swa_s.append((kk[:, -window:], vv[:, -window:]))
        else:
            n_gc = 4
            bbd, cbd, a_re, a_im = _s5_params(ssm_a_re[j], ssm_a_im[j], ssm_b_re[j], ssm_b_im[j],
                                              ssm_c_re[j], ssm_c_im[j], ssm_log_dt[j], n_gc)
            ns = n_groups * n_state

            def run(y2d, nbatch, nlen, h0_re, h0_im):
                x_t = y2d.reshape(nbatch, nlen, d).transpose(1, 0, 2)
                z_t, l_re, l_im = _s5_scan(x_t, norm_g[l, 1], bbd, cbd, a_re, a_im, ssm_d[j], h0_re, h0_im)
                out_t = _glu_residual(z_t.reshape(nlen * nbatch, d), wglu[j], x_t.reshape(nlen * nbatch, d))
                y_new = out_t.reshape(nlen, nbatch, d).transpose(1, 0, 2).reshape(nbatch * nlen, d)
                return y_new, l_re.reshape(nbatch, n_groups, n_state), l_im.reshape(nbatch, n_groups, n_state)

            zeros = jnp.zeros((batch, ns), F32)
            yp, rp, ip = run(yp, batch, seq, zeros, zeros)
            ys, rn, inn = run(ys, n_seq, n_tok, state_ssm_re[j].astype(F32).reshape(n_seq, ns),
                              state_ssm_im[j].astype(F32).reshape(n_seq, ns))
            ssm_p.append((rp, ip))
            ssm_s.append((rn, inn))
        kvm = _norm_matmul(mem2d, norm_g[l, 2], wxkv[l], F32)
        mem_p.append((kvm[:, :xd].reshape(batch, n_mem, x_heads, x_hd), kvm[:, xd:].reshape(batch, n_mem, x_heads, x_hd)))
        kvm4 = kvm.reshape(1, batch, n_mem, 2 * xd)
        qx = _norm_matmul(yp, norm_g[l, 3], wxq[l], BF16)
        ox = _cross_attention(qx, kvm4, kvm4, 0, 1, 0, batch, seq, x_heads, x_hd, BF16)
        yp = _matmul_residual(ox, wxo[l], yp)
        qx = _norm_matmul(ys, norm_g[l, 3], wxq[l], F32)
        ox = _cross_attention(qx, cache_mem_k.reshape(depth, n_seq, n_mem * x_heads, x_hd),
                              cache_mem_v.reshape(depth, n_seq, n_mem * x_heads, x_hd),
                              0, 0, l, n_seq, n_tok, x_heads, x_hd, F32, head_rows=True)
        ys = _matmul_residual(ox, wxo[l], ys)
        yp = _ffn(yp, norm_g[l, 4], wg[l, 1], wu[l, 1], wd[l, 1])
        ys = _ffn(ys, norm_g[l, 4], wg[l, 1], wu[l, 1], wd[l, 1])

    y_prompt = _final_norm(yp, final_g).reshape(batch, seq, d)
    y_sample = _final_norm(ys, final_g).reshape(n_seq, n_tok, d)

    def stack(entries, i):
        return jnp.stack([e[i] for e in entries])

    return (y_prompt, y_sample,
            stack(fox_p, 0), stack(fox_p, 1), stack(fox_p, 2),
            stack(swa_p, 0), stack(swa_p, 1),
            stack(ssm_p, 0), stack(ssm_p, 1),
            stack(mem_p, 0), stack(mem_p, 1),
            stack(fox_s, 0), stack(fox_s, 1), stack(fox_s, 2),
            stack(swa_s, 0), stack(swa_s, 1),
            stack(ssm_s, 0), stack(ssm_s, 1))
```

```python
import functools
import math

import jax
import jax.numpy as jnp
from jax import lax
from jax.experimental import pallas as pl
from jax.experimental.pallas import tpu as pltpu

F32 = jnp.float32
BF16 = jnp.bfloat16

RMS_EPS = 1e-6
NEG_INF = -1e30
ROPE_THETA = 10000.0
SSM_GROUP = 16
GELU_C = math.sqrt(2.0 / math.pi)

V7X_VMEM_BYTES = 64 * 1024 * 1024
LANES = 128
SUBLANES = 8


def _cparams(sem, vmem_mb):
    assert vmem_mb * 2**20 < V7X_VMEM_BYTES
    return pltpu.CompilerParams(dimension_semantics=sem, vmem_limit_bytes=vmem_mb * 2**20)


def _rms(x, g):
    ms = jnp.mean(x * x, axis=-1, keepdims=True)
    return (x * lax.rsqrt(ms + RMS_EPS)) * g


def _dot(a, b):
    return jnp.dot(a, b, preferred_element_type=F32)


def _dot_nt(a, b):
    return lax.dot_general(a, b, (((1,), (1,)), ((), ())), preferred_element_type=F32)


def _dot_exact(a, b):
    return jnp.dot(a, b, preferred_element_type=F32, precision=lax.Precision.HIGHEST)


def _row_tile(m, want):
    t = min(want, m)
    assert m % t == 0, (m, t)
    return t


def _ffn_kernel(x_ref, g_ref, wg_ref, wu_ref, wd_ref, o_ref, *, fc):
    x = x_ref[...]
    h = _rms(x, g_ref[...]).astype(BF16)
    acc = None
    for c in range(wg_ref.shape[1] // fc):
        gate = _dot(h, wg_ref[:, c * fc:(c + 1) * fc])
        up = _dot(h, wu_ref[:, c * fc:(c + 1) * fc])
        act = (gate * jax.nn.sigmoid(gate) * up).astype(BF16)
        part = _dot(act, wd_ref[c * fc:(c + 1) * fc, :])
        acc = part if acc is None else acc + part
    o_ref[...] = x + 0.5 * acc


def _ffn(x, g, wg, wu, wd):
    m, d = x.shape
    ff = wg.shape[1]
    tm = _row_tile(m, 1024)
    fc = 256
    assert ff % fc == 0
    once = pl.Buffered(1)
    return pl.pallas_call(
        functools.partial(_ffn_kernel, fc=fc),
        grid=(m // tm,),
        in_specs=[
            pl.BlockSpec((tm, d), lambda i: (i, 0)),
            pl.BlockSpec((1, d), lambda i: (0, 0)),
            pl.BlockSpec((d, ff), lambda i: (0, 0), pipeline_mode=once),
            pl.BlockSpec((d, ff), lambda i: (0, 0), pipeline_mode=once),
            pl.BlockSpec((ff, d), lambda i: (0, 0), pipeline_mode=once),
        ],
        out_specs=pl.BlockSpec((tm, d), lambda i: (i, 0)),
        out_shape=jax.ShapeDtypeStruct((m, d), F32),
        compiler_params=_cparams(("arbitrary",), 56),
        name="ffn",
    )(x, g.reshape(1, d), wg, wu, wd)


def _nmm_kernel(x_ref, g_ref, w_ref, o_ref):
    h = _rms(x_ref[...], g_ref[...]).astype(BF16)
    o_ref[...] = _dot(h, w_ref[...]).astype(o_ref.dtype)


def _norm_matmul(x, g, w, out_dtype):
    m, d = x.shape
    n = w.shape[1]
    tm = _row_tile(m, 512)
    return pl.pallas_call(
        _nmm_kernel,
        grid=(m // tm,),
        in_specs=[
            pl.BlockSpec((tm, d), lambda i: (i, 0)),
            pl.BlockSpec((1, d), lambda i: (0, 0)),
            pl.BlockSpec((d, n), lambda i: (0, 0)),
        ],
        out_specs=pl.BlockSpec((tm, n), lambda i: (i, 0)),
        out_shape=jax.ShapeDtypeStruct((m, n), out_dtype),
        compiler_params=_cparams(("arbitrary",), 40),
        name="norm_matmul",
    )(x, g.reshape(1, d), w)


def _mmres_kernel(a_ref, w_ref, x_ref, o_ref):
    o_ref[...] = x_ref[...] + _dot(a_ref[...].astype(BF16), w_ref[...])


def _matmul_residual(a, w, x):
    m, k = a.shape
    d = w.shape[1]
    tm = _row_tile(m, 512)
    return pl.pallas_call(
        _mmres_kernel,
        grid=(m // tm,),
        in_specs=[
            pl.BlockSpec((tm, k), lambda i: (i, 0)),
            pl.BlockSpec((k, d), lambda i: (0, 0)),
            pl.BlockSpec((tm, d), lambda i: (i, 0)),
        ],
        out_specs=pl.BlockSpec((tm, d), lambda i: (i, 0)),
        out_shape=jax.ShapeDtypeStruct((m, d), F32),
        compiler_params=_cparams(("arbitrary",), 40),
        name="matmul_residual",
    )(a, w, x)


def _final_norm_kernel(x_ref, g_ref, o_ref):
    o_ref[...] = _rms(x_ref[...], g_ref[...])


def _final_norm(x, g):
    m, d = x.shape
    tm = _row_tile(m, 1024)
    return pl.pallas_call(
        _final_norm_kernel,
        grid=(m // tm,),
        in_specs=[pl.BlockSpec((tm, d), lambda i: (i, 0)), pl.BlockSpec((1, d), lambda i: (0, 0))],
        out_specs=pl.BlockSpec((tm, d), lambda i: (i, 0)),
        out_shape=jax.ShapeDtypeStruct((m, d), F32),
        compiler_params=_cparams(("arbitrary",), 40),
        name="final_norm",
    )(x, g.reshape(1, d))


def _fox_proj_kernel(x_ref, g_ref, w_ref, b_ref, q_ref, k_ref, v_ref, lf_ref, c_ref, carry_ref,
                     *, qd, kd, nh, tiles_per_seq, q_scale):
    i = pl.program_id(0)
    h = _rms(x_ref[...], g_ref[...]).astype(BF16)
    proj = _dot(h, w_ref[...])
    q_ref[...] = (proj[:, :qd] * q_scale).astype(q_ref.dtype)
    k_ref[...] = proj[:, qd:qd + kd]
    v_ref[...] = proj[:, qd + kd:qd + 2 * kd]
    z = proj[:, qd + 2 * kd:] + b_ref[...]
    lf = jnp.minimum(z, 0.0) - jnp.log1p(jnp.exp(-jnp.abs(z)))
    lf_ref[...] = lf[:, :nh]

    @pl.when(i % tiles_per_seq == 0)
    def _():
        carry_ref[...] = jnp.zeros_like(carry_ref)

    tm = lf.shape[0]
    r = lax.broadcasted_iota(jnp.int32, (tm, tm), 0)
    c = lax.broadcasted_iota(jnp.int32, (tm, tm), 1)
    tril = jnp.where(c <= r, 1.0, 0.0).astype(F32)
    csum = _dot_exact(tril, lf) + carry_ref[...]
    c_ref[...] = csum[:, :nh]
    carry_ref[...] = csum[tm - 1:tm, :]


def _fox_project(x, g, w_pad, b_pad, seq_len, *, qd, kd, nh, hd, q_dtype):
    m, d = x.shape
    n = w_pad.shape[1]
    tm = _row_tile(seq_len, 512)
    kern = functools.partial(_fox_proj_kernel, qd=qd, kd=kd, nh=nh, tiles_per_seq=seq_len // tm,
                             q_scale=hd ** -0.5)
    return pl.pallas_call(
        kern,
        grid=(m // tm,),
        in_specs=[
            pl.BlockSpec((tm, d), lambda i: (i, 0)),
            pl.BlockSpec((1, d), lambda i: (0, 0)),
            pl.BlockSpec((d, n), lambda i: (0, 0)),
            pl.BlockSpec((1, LANES), lambda i: (0, 0)),
        ],
        out_specs=[
            pl.BlockSpec((tm, qd), lambda i: (i, 0)),
            pl.BlockSpec((tm, kd), lambda i: (i, 0)),
            pl.BlockSpec((tm, kd), lambda i: (i, 0)),
            pl.BlockSpec((tm, nh), lambda i: (i, 0)),
            pl.BlockSpec((tm, nh), lambda i: (i, 0)),
        ],
        out_shape=[
            jax.ShapeDtypeStruct((m, qd), q_dtype),
            jax.ShapeDtypeStruct((m, kd), F32),
            jax.ShapeDtypeStruct((m, kd), F32),
            jax.ShapeDtypeStruct((m, nh), F32),
            jax.ShapeDtypeStruct((m, nh), F32),
        ],
        scratch_shapes=[pltpu.VMEM((1, LANES), F32)],
        compiler_params=_cparams(("arbitrary",), 48),
        name="fox_project",
    )(x, g.reshape(1, d), w_pad, b_pad)


def _split3(x):
    a = lax.reduce_precision(x, exponent_bits=8, mantissa_bits=7)
    r = x - a
    b = lax.reduce_precision(r, exponent_bits=8, mantissa_bits=7)
    return a.astype(BF16), b.astype(BF16), (r - b).astype(BF16)


def _fox_prompt_kernel(q_ref, k_ref, vt_ref, o_ref, *, tq):
    i = pl.program_id(2)
    q = q_ref[0, 0, 0]
    rows = q.shape[0]

    def step(j, carry, masked):
        m, l, acc = carry
        start = pl.multiple_of(j * tq, tq)
        k_j = k_ref[0, 0, pl.ds(start, tq), :]
        vt_j = vt_ref[0, 0, j]
        s = _dot_nt(k_j, q)
        if masked:
            key = lax.broadcasted_iota(jnp.int32, (tq, rows), 0)
            tok = lax.broadcasted_iota(jnp.int32, (tq, rows), 1) % tq
            s = jnp.where(key <= tok, s, NEG_INF)
        m_new = jnp.maximum(m, jnp.max(s, axis=0, keepdims=True))
        alpha = jnp.exp(m - m_new)
        p = jnp.exp(s - m_new)
        l = alpha * l + jnp.sum(p, axis=0, keepdims=True)
        acc = alpha * acc + _dot(vt_j, p.astype(BF16))
        return m_new, l, acc

    hd = vt_ref.shape[3]
    init = (jnp.full((1, rows), NEG_INF, F32), jnp.zeros((1, rows), F32), jnp.zeros((hd, rows), F32))
    carry = lax.fori_loop(0, i, lambda j, cr: step(j, cr, False), init)
    m, l, acc = step(i, carry, True)
    o_ref[0, 0, 0] = (acc / l).astype(o_ref.dtype)


def _fox_prompt_attention(q, k, v, c, batch, seq_len, n_heads, kv_heads, hd):
    group = n_heads // kv_heads
    tq = _row_tile(seq_len, 256)
    nq = seq_len // tq
    nterm = 3
    assert hd + 2 * nterm * group <= LANES
    c1, c2, c3 = _split3(c.reshape(batch, seq_len, kv_heads, group))
    one = jnp.ones_like(c1)
    eye = jnp.eye(group, dtype=BF16)
    q_bias = jnp.stack([c1, c2, c3, one, one, one], axis=-1)
    q_bias = (q_bias[:, :, :, :, None, :] * eye[None, None, None, :, :, None]).reshape(
        batch, seq_len, kv_heads, group, 2 * nterm * group)
    k_bias = jnp.stack([one, one, one, -c1, -c2, -c3], axis=-1).reshape(
        batch, seq_len, kv_heads, 2 * nterm * group)
    pad = LANES - hd - 2 * nterm * group
    q_aug = jnp.concatenate([q.reshape(batch, seq_len, kv_heads, group, hd), q_bias,
                             jnp.zeros((batch, seq_len, kv_heads, group, pad), BF16)], axis=-1)
    k_aug = jnp.concatenate([k.astype(BF16).reshape(batch, seq_len, kv_heads, hd), k_bias,
                             jnp.zeros((batch, seq_len, kv_heads, pad), BF16)], axis=-1)
    q_r = (q_aug.reshape(batch, nq, tq, kv_heads, group, LANES).transpose(0, 3, 1, 4, 2, 5)
           .reshape(batch, kv_heads, nq, group * tq, LANES))
    k_r = k_aug.transpose(0, 2, 1, 3)
    vt_r = v.astype(BF16).reshape(batch, nq, tq, kv_heads, hd).transpose(0, 3, 1, 4, 2)
    kern = functools.partial(_fox_prompt_kernel, tq=tq)
    o = pl.pallas_call(
        kern,
        grid=(batch, kv_heads, nq),
        in_specs=[
            pl.BlockSpec((1, 1, 1, group * tq, LANES), lambda b, kv, i: (b, kv, i, 0, 0)),
            pl.BlockSpec((1, 1, seq_len, LANES), lambda b, kv, i: (b, kv, 0, 0)),
            pl.BlockSpec((1, 1, nq, hd, tq), lambda b, kv, i: (b, kv, 0, 0, 0)),
        ],
        out_specs=pl.BlockSpec((1, 1, 1, hd, group * tq), lambda b, kv, i: (b, kv, i, 0, 0)),
        out_shape=jax.ShapeDtypeStruct((batch, kv_heads, nq, hd, group * tq), BF16),
        compiler_params=_cparams(("arbitrary", "arbitrary", "arbitrary"), 40),
        name="fox_prompt_attention",
    )(q_r, k_r, vt_r)
    return (o.reshape(batch, kv_heads, nq, hd, group, tq).transpose(0, 2, 5, 1, 4, 3)
            .reshape(batch * seq_len, n_heads * hd))


def _fox_sample_kernel(pt_ref, q_ref, knt_ref, vnt_ref, lfr_ref, kpool, vpool, fpool, o_ref,
                       kbuf, vbuf, fbuf, sem, *, layer, n_seq, cpp, n_chunks, page, n_tok, heads, hd, kv_heads):
    b = pl.program_id(0)
    rows = heads * n_tok
    ckeys = cpp * page

    def copies(seq, chunk, slot):
        out = []
        for pi in range(cpp):
            pg = pt_ref[seq, chunk * cpp + pi]
            lanes_pi = pl.ds(pi * page, page)
            out.append(pltpu.make_async_copy(kpool.at[layer, pg], kbuf.at[slot, :, lanes_pi], sem.at[slot]))
            out.append(pltpu.make_async_copy(vpool.at[layer, pg], vbuf.at[slot, :, lanes_pi], sem.at[slot]))
            out.append(pltpu.make_async_copy(fpool.at[layer, pg], fbuf.at[slot, :, lanes_pi], sem.at[slot]))
        return out

    n_slots = kbuf.shape[0]
    total = n_seq * n_chunks

    def chunk_copies(g):
        seq = g // n_chunks
        return copies(seq, n_chunks - 1 - (g - seq * n_chunks), g % n_slots)

    def start(g):
        for cp in chunk_copies(g):
            cp.start()

    def wait(g):
        for cp in chunk_copies(g):
            cp.wait()

    @pl.when(b == 0)
    def _():
        for g0 in range(min(n_slots - 1, total)):
            start(g0)

    q = q_ref[0]
    lfr = lfr_ref[0]
    lane = lax.broadcasted_iota(jnp.int32, (rows, page), 1)
    trow = lax.broadcasted_iota(jnp.int32, (rows, page), 0) % n_tok
    ckn = lfr
    k = 1
    while k < n_tok:
        ckn = ckn + jnp.where(lane >= k, pltpu.roll(ckn, k, 1), 0.0)
        k *= 2
    cq = jnp.sum(jnp.where(lane == trow, ckn, 0.0), axis=1, keepdims=True)

    s = _dot(q, knt_ref[0].astype(BF16))
    s = jnp.where(lane <= trow, s + (cq - ckn), NEG_INF)
    m0 = jnp.max(s, axis=1, keepdims=True)
    p = jnp.exp(s - m0)
    l0 = jnp.sum(p, axis=1, keepdims=True)
    acc0 = _dot_nt(p.astype(BF16), vnt_ref[0].astype(BF16))

    lane_c = lax.broadcasted_iota(jnp.int32, (heads, ckeys), 1)

    def chunk_body(ci, carry):
        m, l, acc, tail = carry
        g = b * n_chunks + ci
        slot = g % n_slots
        wait(g)

        @pl.when(g + (n_slots - 1) < total)
        def _():
            start(g + (n_slots - 1))

        f = fbuf[slot]
        r = f
        k = 1
        while k < ckeys:
            r = r + jnp.where(lane_c < ckeys - k, pltpu.roll(r, ckeys - k, 1), 0.0)
            k *= 2
        suf = (r - f) + tail
        bias = jnp.concatenate(
            [jnp.broadcast_to(suf[hh:hh + 1, :], (n_tok, ckeys)) for hh in range(heads)], axis=0)
        s = _dot(q, kbuf[slot].astype(BF16)) + (bias + cq)
        m_new = jnp.maximum(m, jnp.max(s, axis=1, keepdims=True))
        alpha = jnp.exp(m - m_new)
        p = jnp.exp(s - m_new)
        l = alpha * l + jnp.sum(p, axis=1, keepdims=True)
        acc = alpha * acc + _dot_nt(p.astype(BF16), vbuf[slot].astype(BF16))
        return m_new, l, acc, tail + r[:, 0:1]

    init = (m0, l0, acc0, jnp.zeros((heads, 1), F32))
    m, l, acc, _ = lax.fori_loop(0, n_chunks, chunk_body, init)
    o = acc / l
    rpk = rows // kv_heads
    o_ref[0] = jnp.concatenate([o[kk * rpk:(kk + 1) * rpk, kk * hd:(kk + 1) * hd] for kk in range(kv_heads)],
                               axis=0)


def _fox_sample_attention(q, k_new, v_new, logf_new, k_pools, v_pools, f_pools, layer, page_table,
                          n_seq, n_tok, n_heads, kv_heads, hd):
    group = n_heads // kv_heads
    n_layers, n_pool, page = k_pools.shape[0], k_pools.shape[1], k_pools.shape[2]
    n_pages = page_table.shape[1]
    width = kv_heads * hd
    rows = n_heads * n_tok
    cpp = 8 if n_pages % 8 == 0 else 1
    assert page % LANES == 0 and n_tok <= page
    n_chunks = n_pages // cpp
    q_t = q.reshape(n_seq, n_tok, kv_heads, group, hd).transpose(0, 2, 3, 1, 4).reshape(n_seq, kv_heads, group * n_tok, hd)
    eye = jnp.eye(kv_heads, dtype=F32)
    q_bd = (q_t[:, :, :, None, :] * eye[None, :, None, :, None]).reshape(n_seq, rows, width).astype(BF16)
    pad = ((0, 0), (0, 0), (0, page - n_tok))
    lfr = jnp.broadcast_to(logf_new.reshape(n_seq, n_tok, n_heads).transpose(0, 2, 1)[:, :, None, :],
                           (n_seq, n_heads, n_tok, n_tok)).reshape(n_seq, rows, n_tok)
    lfr = jnp.pad(lfr, pad)
    knt = jnp.pad(k_new.reshape(n_seq, n_tok, width).transpose(0, 2, 1), pad)
    vnt = jnp.pad(v_new.reshape(n_seq, n_tok, width).transpose(0, 2, 1), pad)
    kp = k_pools.transpose(0, 1, 3, 4, 2).reshape(n_layers, n_pool, width, page)
    vp = v_pools.transpose(0, 1, 3, 4, 2).reshape(n_layers, n_pool, width, page)
    fp = f_pools.transpose(0, 1, 3, 2)
    n_slots = 4
    kern = functools.partial(_fox_sample_kernel, layer=layer, n_seq=n_seq, cpp=cpp, n_chunks=n_chunks, page=page,
                             n_tok=n_tok, heads=n_heads, hd=hd, kv_heads=kv_heads)
    grid_spec = pltpu.PrefetchScalarGridSpec(
        num_scalar_prefetch=1,
        grid=(n_seq,),
        in_specs=[
            pl.BlockSpec((1, rows, width), lambda b, pt: (b, 0, 0)),
            pl.BlockSpec((1, width, page), lambda b, pt: (b, 0, 0)),
            pl.BlockSpec((1, width, page), lambda b, pt: (b, 0, 0)),
            pl.BlockSpec((1, rows, page), lambda b, pt: (b, 0, 0)),
            pl.BlockSpec(memory_space=pl.ANY),
            pl.BlockSpec(memory_space=pl.ANY),
            pl.BlockSpec(memory_space=pl.ANY),
        ],
        out_specs=pl.BlockSpec((1, rows, hd), lambda b, pt: (b, 0, 0)),
        scratch_shapes=[
            pltpu.VMEM((n_slots, width, cpp * page), F32),
            pltpu.VMEM((n_slots, width, cpp * page), F32),
            pltpu.VMEM((n_slots, n_heads, cpp * page), F32),
            pltpu.SemaphoreType.DMA((n_slots,)),
        ],
    )
    o = pl.pallas_call(
        kern,
        grid_spec=grid_spec,
        out_shape=jax.ShapeDtypeStruct((n_seq, rows, hd), F32),
        compiler_params=_cparams(("arbitrary",), 40),
        name="fox_sample_attention",
    )(page_table, q_bd, knt, vnt, lfr, kp, vp, fp)
    return (o.reshape(n_seq, kv_heads, group, n_tok, hd).transpose(0, 3, 1, 2, 4)
            .reshape(n_seq * n_tok, n_heads * hd))


def _swa_proj_kernel(x_ref, g_ref, w_ref, cos_ref, sin_ref, q_ref, k_ref, v_ref, *, qd, kd, hd, q_scale):
    h = _rms(x_ref[...], g_ref[...]).astype(BF16)
    proj = _dot(h, w_ref[...])
    cos = cos_ref[...]
    sin = sin_ref[...]
    half = hd // 2

    def rope(x):
        n = x.shape[1]
        reps = n // LANES
        cs = jnp.concatenate([cos] * reps, axis=1) if reps > 1 else cos
        sn = jnp.concatenate([sin] * reps, axis=1) if reps > 1 else sin
        lane = lax.broadcasted_iota(jnp.int32, x.shape, 1) % hd
        up = pltpu.roll(x, n - half, 1)
        dn = pltpu.roll(x, half, 1)
        return x * cs + jnp.where(lane < half, up, dn) * sn

    q_ref[...] = (rope(proj[:, :qd]) * q_scale).astype(q_ref.dtype)
    k_ref[...] = rope(proj[:, qd:qd + kd])
    v_ref[...] = proj[:, qd + kd:qd + 2 * kd]


def _swa_project(x, g, w, cos, sin, *, qd, kd, hd, q_dtype):
    m, d = x.shape
    n = w.shape[1]
    tm = _row_tile(m, 512)
    kern = functools.partial(_swa_proj_kernel, qd=qd, kd=kd, hd=hd, q_scale=hd ** -0.5)
    return pl.pallas_call(
        kern,
        grid=(m // tm,),
        in_specs=[
            pl.BlockSpec((tm, d), lambda i: (i, 0)),
            pl.BlockSpec((1, d), lambda i: (0, 0)),
            pl.BlockSpec((d, n), lambda i: (0, 0)),
            pl.BlockSpec((tm, LANES), lambda i: (i, 0)),
            pl.BlockSpec((tm, LANES), lambda i: (i, 0)),
        ],
        out_specs=[
            pl.BlockSpec((tm, qd), lambda i: (i, 0)),
            pl.BlockSpec((tm, kd), lambda i: (i, 0)),
            pl.BlockSpec((tm, kd), lambda i: (i, 0)),
        ],
        out_shape=[
            jax.ShapeDtypeStruct((m, qd), q_dtype),
            jax.ShapeDtypeStruct((m, kd), F32),
            jax.ShapeDtypeStruct((m, kd), F32),
        ],
        compiler_params=_cparams(("arbitrary",), 48),
        name="swa_project",
    )(x, g.reshape(1, d), w, cos, sin)


def _local_attn_kernel(q_ref, kp_ref, vp_ref, kc_ref, vc_ref, sink_ref, o_ref, *, tq, window, first_block_has_prev):
    i = pl.program_id(2)
    q = q_ref[0, 0, 0].astype(BF16)
    rows = q.shape[0]
    kp = kp_ref[0, 0].astype(BF16)
    vp = vp_ref[0, 0].astype(BF16)
    kc = kc_ref[0, 0].astype(BF16)
    vc = vc_ref[0, 0].astype(BF16)
    w = kp.shape[0]

    def dot_keys(v, p):
        return lax.dot_general(v, p, (((0,), (0,)), ((), ())), preferred_element_type=F32)

    j_p = lax.broadcasted_iota(jnp.int32, (w, rows), 0)
    t_p = lax.broadcasted_iota(jnp.int32, (w, rows), 1) % tq
    ok_p = (w + t_p - j_p) <= window
    if not first_block_has_prev:
        ok_p = jnp.logical_and(ok_p, i > 0)
    j_c = lax.broadcasted_iota(jnp.int32, (tq, rows), 0)
    t_c = lax.broadcasted_iota(jnp.int32, (tq, rows), 1) % tq
    ok_c = j_c <= t_c
    s_p = jnp.where(ok_p, _dot_nt(kp, q), NEG_INF)
    s_c = jnp.where(ok_c, _dot_nt(kc, q), NEG_INF)
    sink = sink_ref[0, 0]
    m = jnp.maximum(jnp.maximum(jnp.max(s_p, axis=0, keepdims=True), jnp.max(s_c, axis=0, keepdims=True)), sink)
    p_p = jnp.exp(s_p - m)
    p_c = jnp.exp(s_c - m)
    l = jnp.sum(p_p, axis=0, keepdims=True) + jnp.sum(p_c, axis=0, keepdims=True) + jnp.exp(sink - m)
    acc = dot_keys(vp, p_p.astype(BF16)) + dot_keys(vc, p_c.astype(BF16))
    o_ref[0, 0, 0] = (acc / l).astype(o_ref.dtype)


def _local_attention(q_r, k_prev, v_prev, k_cur, v_cur, sink_rows, *, tq, window, prev_is_cur, out_dtype):
    b, kvh, nb, rows, hd = q_r.shape
    w = tq if prev_is_cur else k_prev.shape[2]
    if prev_is_cur:
        prev_map = lambda bb, kv, i: (bb, kv, jnp.maximum(i - 1, 0), 0)
    else:
        assert nb == 1
        prev_map = lambda bb, kv, i: (bb, kv, 0, 0)
    kern = functools.partial(_local_attn_kernel, tq=tq, window=window, first_block_has_prev=not prev_is_cur)
    return pl.pallas_call(
        kern,
        grid=(b, kvh, nb),
        in_specs=[
            pl.BlockSpec((1, 1, 1, rows, hd), lambda bb, kv, i: (bb, kv, i, 0, 0)),
            pl.BlockSpec((1, 1, w, hd), prev_map),
            pl.BlockSpec((1, 1, w, hd), prev_map),
            pl.BlockSpec((1, 1, tq, hd), lambda bb, kv, i: (bb, kv, i, 0)),
            pl.BlockSpec((1, 1, tq, hd), lambda bb, kv, i: (bb, kv, i, 0)),
            pl.BlockSpec((1, 1, 1, rows), lambda bb, kv, i: (0, kv, 0, 0)),
        ],
        out_specs=pl.BlockSpec((1, 1, 1, hd, rows), lambda bb, kv, i: (bb, kv, i, 0, 0)),
        out_shape=jax.ShapeDtypeStruct((b, kvh, nb, hd, rows), out_dtype),
        compiler_params=_cparams(("arbitrary", "arbitrary", "arbitrary"), 40),
        name="local_attention",
    )(q_r, k_prev, v_prev, k_cur, v_cur, sink_rows)


def _group_rows(q, batch, nb, tq, kv_heads, group, hd):
    return (q.reshape(batch, nb, tq, kv_heads, group, hd).transpose(0, 3, 1, 4, 2, 5)
            .reshape(batch, kv_heads, nb, group * tq, hd))


def _ungroup_rows(o, batch, nb, tq, kv_heads, group, hd):
    return (o.reshape(batch, kv_heads, nb, hd, group, tq).transpose(0, 2, 5, 1, 4, 3)
            .reshape(batch * nb * tq, kv_heads * group * hd))


def _cross_attn_kernel(q_ref, k_ref, v_ref, o_ref, *, heads, hd, scale, head_rows):
    outs = []
    for h in range(heads):
        q = q_ref[0, :, h * hd:(h + 1) * hd].astype(BF16)
        if head_rows:
            n_mem = k_ref.shape[2] // heads
            k = k_ref[0, 0, pl.ds(h, n_mem, stride=heads), :].astype(BF16)
            v = v_ref[0, 0, pl.ds(h, n_mem, stride=heads), :].astype(BF16)
        else:
            k = k_ref[0, 0, :, h * hd:(h + 1) * hd].astype(BF16)
            v = v_ref[0, 0, :, h * hd:(h + 1) * hd].astype(BF16)
        s = _dot_nt(q, k) * scale
        m = jnp.max(s, axis=1, keepdims=True)
        p = jnp.exp(s - m)
        l = jnp.sum(p, axis=1, keepdims=True)
        outs.append(_dot(p.astype(BF16), v) / l)
    o_ref[0] = jnp.concatenate(outs, axis=1).astype(o_ref.dtype)


def _cross_attention(q, mk, mv, k_col, v_col, layer, batch, seq_len, heads, hd, out_dtype, head_rows=False):
    width = heads * hd
    tq = _row_tile(seq_len, 512)
    nq = seq_len // tq
    kv_block = (1, 1, mk.shape[2], hd if head_rows else width)
    kern = functools.partial(_cross_attn_kernel, heads=heads, hd=hd, scale=hd ** -0.5, head_rows=head_rows)
    o = pl.pallas_call(
        kern,
        grid=(batch, nq),
        in_specs=[
            pl.BlockSpec((1, tq, width), lambda b, i: (b, i, 0)),
            pl.BlockSpec(kv_block, lambda b, i: (layer, b, 0, k_col)),
            pl.BlockSpec(kv_block, lambda b, i: (layer, b, 0, v_col)),
        ],
        out_specs=pl.BlockSpec((1, tq, width), lambda b, i: (b, i, 0)),
        out_shape=jax.ShapeDtypeStruct((batch, seq_len, width), out_dtype),
        compiler_params=_cparams(("arbitrary", "arbitrary"), 40),
        name="cross_attention",
    )(q.reshape(batch, seq_len, width), mk, mv)
    return o.reshape(batch * seq_len, width)


def _s5_kernel(x_ref, g_ref, bbd_ref, cbd_ref, are_ref, aim_ref, d_ref, h0re_ref, h0im_ref,
               z_ref, lre_ref, lim_ref, st_ref, sre_ref, sim_ref, *, tl, n_gc, gc_in, gc_st, cw):
    li = pl.program_id(1)

    @pl.when(li == 0)
    def _():
        sre_ref[...] = h0re_ref[...]
        sim_ref[...] = h0im_ref[...]

    d_model = x_ref.shape[2]
    rows = tl * SUBLANES
    x = x_ref[...].reshape(rows, d_model)
    h = _rms(x, g_ref[...])
    hb = h.astype(BF16)
    for k in range(n_gc):
        st_ref[:, k * 2 * gc_st:(k + 1) * 2 * gc_st] = _dot(hb[:, k * gc_in:(k + 1) * gc_in], bbd_ref[k])

    for k in range(n_gc):
        for c in range(gc_st // cw):
            re0 = k * 2 * gc_st + c * cw
            im0 = re0 + gc_st
            sc = k * gc_st + c * cw
            ar = jnp.broadcast_to(are_ref[:, sc:sc + cw], (SUBLANES, cw))
            ai = jnp.broadcast_to(aim_ref[:, sc:sc + cw], (SUBLANES, cw))

            def body(l, carry, re0=re0, im0=im0, ar=ar, ai=ai):
                sr, si = carry
                r0 = pl.multiple_of(l * SUBLANES, SUBLANES)
                br = st_ref[pl.ds(r0, SUBLANES), re0:re0 + cw]
                bi = st_ref[pl.ds(r0, SUBLANES), im0:im0 + cw]
                nr = ar * sr - ai * si + br
                ni = ar * si + ai * sr + bi
                st_ref[pl.ds(r0, SUBLANES), re0:re0 + cw] = nr
                st_ref[pl.ds(r0, SUBLANES), im0:im0 + cw] = ni
                return nr, ni

            sr, si = lax.fori_loop(0, tl, body, (sre_ref[:, sc:sc + cw], sim_ref[:, sc:sc + cw]))
            sre_ref[:, sc:sc + cw] = sr
            sim_ref[:, sc:sc + cw] = si

    ys = [_dot(st_ref[:, k * 2 * gc_st:(k + 1) * 2 * gc_st].astype(BF16), cbd_ref[k]) for k in range(n_gc)]
    y = jnp.concatenate(ys, axis=1) + d_ref[...] * h
    z = y * (0.5 * (1.0 + jnp.tanh(GELU_C * (y + 0.044715 * (y * y * y)))))
    z_ref[...] = z.reshape(tl, SUBLANES, d_model)

    @pl.when(li == pl.num_programs(1) - 1)
    def _():
        lre_ref[...] = sre_ref[...]
        lim_ref[...] = sim_ref[...]


def _s5_scan(x_t, g, bbd, cbd, a_re, a_im, d_skip, h0_re, h0_im):
    seq_len, bt, d = x_t.shape
    n_gc, gc_in, two_gc_st = bbd.shape
    gc_st = two_gc_st // 2
    ns = n_gc * gc_st
    tl = _row_tile(seq_len, 32)
    kern = functools.partial(_s5_kernel, tl=tl, n_gc=n_gc, gc_in=gc_in, gc_st=gc_st, cw=512)
    return pl.pallas_call(
        kern,
        grid=(bt // SUBLANES, seq_len // tl),
        in_specs=[
            pl.BlockSpec((tl, SUBLANES, d), lambda b, l: (l, b, 0)),
            pl.BlockSpec((1, d), lambda b, l: (0, 0)),
            pl.BlockSpec((n_gc, gc_in, 2 * gc_st), lambda b, l: (0, 0, 0)),
            pl.BlockSpec((n_gc, 2 * gc_st, gc_in), lambda b, l: (0, 0, 0)),
            pl.BlockSpec((1, ns), lambda b, l: (0, 0)),
            pl.BlockSpec((1, ns), lambda b, l: (0, 0)),
            pl.BlockSpec((1, d), lambda b, l: (0, 0)),
            pl.BlockSpec((SUBLANES, ns), lambda b, l: (b, 0)),
            pl.BlockSpec((SUBLANES, ns), lambda b, l: (b, 0)),
        ],
        out_specs=[
            pl.BlockSpec((tl, SUBLANES, d), lambda b, l: (l, b, 0)),
            pl.BlockSpec((SUBLANES, ns), lambda b, l: (b, 0)),
            pl.BlockSpec((SUBLANES, ns), lambda b, l: (b, 0)),
        ],
        out_shape=[
            jax.ShapeDtypeStruct((seq_len, bt, d), F32),
            jax.ShapeDtypeStruct((bt, ns), F32),
            jax.ShapeDtypeStruct((bt, ns), F32),
        ],
        scratch_shapes=[
            pltpu.VMEM((tl * SUBLANES, 2 * ns), F32),
            pltpu.VMEM((SUBLANES, ns), F32),
            pltpu.VMEM((SUBLANES, ns), F32),
        ],
        compiler_params=_cparams(("arbitrary", "arbitrary"), 48),
        name="s5_scan",
    )(x_t, g.reshape(1, d), bbd, cbd, a_re, a_im, d_skip.reshape(1, d), h0_re, h0_im)


def _glu_kernel(z_ref, w_ref, x_ref, o_ref):
    zw = _dot(z_ref[...].astype(BF16), w_ref[...])
    d = x_ref.shape[1]
    o_ref[...] = x_ref[...] + zw[:, :d] * jax.nn.sigmoid(zw[:, d:])


def _glu_residual(z, w, x):
    m, d = x.shape
    tm = _row_tile(m, 512)
    return pl.pallas_call(
        _glu_kernel,
        grid=(m // tm,),
        in_specs=[
            pl.BlockSpec((tm, d), lambda i: (i, 0)),
            pl.BlockSpec((d, 2 * d), lambda i: (0, 0)),
            pl.BlockSpec((tm, d), lambda i: (i, 0)),
        ],
        out_specs=pl.BlockSpec((tm, d), lambda i: (i, 0)),
        out_shape=jax.ShapeDtypeStruct((m, d), F32),
        compiler_params=_cparams(("arbitrary",), 48),
        name="glu_residual",
    )(z, w, x)


def _s5_params(a_re, a_im, b_re, b_im, c_re, c_im, log_dt, n_gc):
    groups, n_state = a_re.shape
    gsz = b_re.shape[2]
    a = lax.complex(a_re.astype(F32), a_im.astype(F32))
    dt = jnp.exp(log_dt.astype(F32))[:, None]
    a_bar = jnp.exp(a * dt)
    b_bar = ((a_bar - 1.0) / a)[..., None] * lax.complex(b_re.astype(F32), b_im.astype(F32))
    gpc = groups // n_gc
    eye = jnp.eye(gpc, dtype=F32)

    def in_blockdiag(m):
        m = m.reshape(n_gc, gpc, n_state, gsz).transpose(0, 1, 3, 2)
        return (m[:, :, :, None, :] * eye[None, :, None, :, None]).reshape(n_gc, gpc * gsz, gpc * n_state)

    def out_blockdiag(m):
        m = m.reshape(n_gc, gpc, gsz, n_state).transpose(0, 1, 3, 2)
        return (m[:, :, :, None, :] * eye[None, :, None, :, None]).reshape(n_gc, gpc * n_state, gpc * gsz)

    bbd = jnp.concatenate([in_blockdiag(b_bar.real), in_blockdiag(b_bar.imag)], axis=2).astype(BF16)
    cbd = jnp.concatenate([out_blockdiag(c_re.astype(F32)), -out_blockdiag(c_im.astype(F32))], axis=1).astype(BF16)
    return bbd, cbd, a_bar.real.reshape(1, -1), a_bar.imag.reshape(1, -1)


def kernel(x_prompt, x_sample, mem_prompt, cache_fox_k, cache_fox_v, cache_fox_logf, state_swa_k, state_swa_v, state_ssm_re, state_ssm_im, cache_mem_k, cache_mem_v, page_table, norm_g, final_g, w_ffn_gate, w_ffn_up, w_ffn_down, w_xq, w_xkv, w_xo, w_fox_qkvf, b_fox_f, w_fox_o, w_swa_qkv, swa_sink, w_swa_o, ssm_a_re, ssm_a_im, ssm_b_re, ssm_b_im, ssm_c_re, ssm_c_im, ssm_d, ssm_log_dt, w_ssm_glu):
    batch, seq, d = x_prompt.shape
    n_seq, n_tok, _ = x_sample.shape
    depth = norm_g.shape[0]
    n_mem = mem_prompt.shape[1]
    fox_kvh, fox_hd = cache_fox_k.shape[3], cache_fox_k.shape[4]
    fox_heads = cache_fox_logf.shape[3]
    fox_qd, fox_kd = fox_heads * fox_hd, fox_kvh * fox_hd
    swa_kvh, swa_hd = state_swa_k.shape[3], state_swa_k.shape[4]
    swa_heads = swa_sink.shape[1]
    swa_qd, swa_kd = swa_heads * swa_hd, swa_kvh * swa_hd
    swa_group = swa_heads // swa_kvh
    window = state_swa_k.shape[2]
    x_heads, x_hd = cache_mem_k.shape[3], cache_mem_k.shape[4]
    xd = x_heads * x_hd
    past_len = page_table.shape[1] * cache_fox_k.shape[2]
    n_groups, n_state = ssm_a_re.shape[1], ssm_a_re.shape[2]

    yp = x_prompt.reshape(batch * seq, d)
    ys = x_sample.reshape(n_seq * n_tok, d)
    mem2d = mem_prompt.reshape(batch * n_mem, d)

    wg = w_ffn_gate.astype(BF16)
    wu = w_ffn_up.astype(BF16)
    wd = w_ffn_down.astype(BF16)
    wxq = w_xq.astype(BF16)
    wxkv = w_xkv.astype(BF16)
    wxo = w_xo.astype(BF16)
    wfox = jnp.pad(w_fox_qkvf, ((0, 0), (0, 0), (0, LANES - fox_heads))).astype(BF16)
    bfox = jnp.pad(b_fox_f, ((0, 0), (0, LANES - fox_heads))).astype(F32)
    wfoxo = w_fox_o.astype(BF16)
    wswa = w_swa_qkv.astype(BF16)
    wswao = w_swa_o.astype(BF16)
    wglu = w_ssm_glu.astype(BF16)

    half = swa_hd // 2
    inv_freq = ROPE_THETA ** (-jnp.arange(half, dtype=F32) / half)

    def rope_tables(pos, reps):
        ang = pos.astype(F32)[:, None] * inv_freq[None, :]
        cos = jnp.cos(ang)
        sin = jnp.sin(ang)
        cos_t = jnp.tile(jnp.concatenate([cos, cos], axis=1), (reps, LANES // swa_hd))
        sin_t = jnp.tile(jnp.concatenate([-sin, sin], axis=1), (reps, LANES // swa_hd))
        return cos_t, sin_t

    fox_p, fox_s, swa_p, swa_s, ssm_p, ssm_s, mem_p = [], [], [], [], [], [], []
    for l in range(depth):
        kind, j = l % 3, l // 3
        yp = _ffn(yp, norm_g[l, 0], wg[l, 0], wu[l, 0], wd[l, 0])
        ys = _ffn(ys, norm_g[l, 0], wg[l, 0], wu[l, 0], wd[l, 0])
        if kind == 0:
            qp, kp, vp, lfp, cp = _fox_project(yp, norm_g[l, 1], wfox[j], bfox[j:j + 1], seq,
                                               qd=fox_qd, kd=fox_kd, nh=fox_heads, hd=fox_hd, q_dtype=BF16)
            op = _fox_prompt_attention(qp, kp, vp, cp, batch, seq, fox_heads, fox_kvh, fox_hd)
            yp = _matmul_residual(op, wfoxo[j], yp)
            qs, ks, vs, lfs, _ = _fox_project(ys, norm_g[l, 1], wfox[j], bfox[j:j + 1], n_seq * n_tok,
                                              qd=fox_qd, kd=fox_kd, nh=fox_heads, hd=fox_hd, q_dtype=F32)
            os_ = _fox_sample_attention(qs, ks, vs, lfs, cache_fox_k, cache_fox_v, cache_fox_logf, j,
                                        page_table, n_seq, n_tok, fox_heads, fox_kvh, fox_hd)
            ys = _matmul_residual(os_, wfoxo[j], ys)
            fox_p.append((kp.reshape(batch, seq, fox_kvh, fox_hd), vp.reshape(batch, seq, fox_kvh, fox_hd),
                          lfp.reshape(batch, seq, fox_heads)))
            fox_s.append((ks.reshape(n_seq, n_tok, fox_kvh, fox_hd), vs.reshape(n_seq, n_tok, fox_kvh, fox_hd),
                          lfs.reshape(n_seq, n_tok, fox_heads)))
        elif kind == 1:
            sink_rows = jnp.broadcast_to(swa_sink[j].astype(F32).reshape(1, swa_kvh, swa_group, 1, 1),
                                         (1, swa_kvh, swa_group, window, 1))
            cos_p, sin_p = rope_tables(jnp.arange(seq), batch)
            qp, kp, vp = _swa_project(yp, norm_g[l, 1], wswa[j], cos_p, sin_p,
                                      qd=swa_qd, kd=swa_kd, hd=swa_hd, q_dtype=BF16)
            nb = seq // window
            q_r = _group_rows(qp, batch, nb, window, swa_kvh, swa_group, swa_hd)
            k_r = kp.reshape(batch, seq, swa_kvh, swa_hd).transpose(0, 2, 1, 3)
            v_r = vp.reshape(batch, seq, swa_kvh, swa_hd).transpose(0, 2, 1, 3)
            o_r = _local_attention(q_r, k_r, v_r, k_r, v_r, sink_rows.reshape(1, swa_kvh, 1, swa_group * window),
                                   tq=window, window=window, prev_is_cur=True, out_dtype=BF16)
            op = _ungroup_rows(o_r, batch, nb, window, swa_kvh, swa_group, swa_hd)
            yp = _matmul_residual(op, wswao[j], yp)
            k4 = kp.reshape(batch, seq, swa_kvh, swa_hd)
            v4 = vp.reshape(batch, seq, swa_kvh, swa_hd)
            n_keep = min(window, seq)
            swa_p.append((k4[:, seq - n_keep:], v4[:, seq - n_keep:]))
            cos_s, sin_s = rope_tables(past_len + jnp.arange(n_tok), n_seq)
            qs, ks, vs = _swa_project(ys, norm_g[l, 1], wswa[j], cos_s, sin_s,
                                      qd=swa_qd, kd=swa_kd, hd=swa_hd, q_dtype=F32)
            q_r = _group_rows(qs, n_seq, 1, n_tok, swa_kvh, swa_group, swa_hd)
            kb = state_swa_k[j].transpose(0, 2, 1, 3)
            vb = state_swa_v[j].transpose(0, 2, 1, 3)
            k_r = ks.reshape(n_seq, n_tok, swa_kvh, swa_hd).transpose(0, 2, 1, 3)
            v_r = vs.reshape(n_seq, n_tok, swa_kvh, swa_hd).transpose(0, 2, 1, 3)
            sink_s = sink_rows[:, :, :, :n_tok].reshape(1, swa_kvh, 1, swa_group * n_tok)
            o_r = _local_attention(q_r, kb, vb, k_r, v_r, sink_s, tq=n_tok, window=window, prev_is_cur=False, out_dtype=F32)
            os_ = _ungroup_rows(o_r, n_seq, 1, n_tok, swa_kvh, swa_group, swa_hd)
            ys = _matmul_residual(os_, wswao[j], ys)
            kk = jnp.concatenate([state_swa_k[j], ks.reshape(n_seq, n_tok, swa_kvh, swa_hd)], axis=1)
            vv = jnp.concatenate([state_swa_v[j], vs.reshape(n_seq, n_tok, swa_kvh, swa_hd)], axis=1)
            swa_s.append((kk[:, -window:], vv[:, -window:]))
        else:
            n_gc = 4
            bbd, cbd, a_re, a_im = _s5_params(ssm_a_re[j], ssm_a_im[j], ssm_b_re[j], ssm_b_im[j],
                                              ssm_c_re[j], ssm_c_im[j], ssm_log_dt[j], n_gc)
            ns = n_groups * n_state

            def run(y2d, nbatch, nlen, h0_re, h0_im):
                x_t = y2d.reshape(nbatch, nlen, d).transpose(1, 0, 2)
                z_t, l_re, l_im = _s5_scan(x_t, norm_g[l, 1], bbd, cbd, a_re, a_im, ssm_d[j], h0_re, h0_im)
                out_t = _glu_residual(z_t.reshape(nlen * nbatch, d), wglu[j], x_t.reshape(nlen * nbatch, d))
                y_new = out_t.reshape(nlen, nbatch, d).transpose(1, 0, 2).reshape(nbatch * nlen, d)
                return y_new, l_re.reshape(nbatch, n_groups, n_state), l_im.reshape(nbatch, n_groups, n_state)

            zeros = jnp.zeros((batch, ns), F32)
            yp, rp, ip = run(yp, batch, seq, zeros, zeros)
            ys, rn, inn = run(ys, n_seq, n_tok, state_ssm_re[j].astype(F32).reshape(n_seq, ns),
                              state_ssm_im[j].astype(F32).reshape(n_seq, ns))
            ssm_p.append((rp, ip))
            ssm_s.append((rn, inn))
        kvm = _norm_matmul(mem2d, norm_g[l, 2], wxkv[l], F32)
        mem_p.append((kvm[:, :xd].reshape(batch, n_mem, x_heads, x_hd), kvm[:, xd:].reshape(batch, n_mem, x_heads, x_hd)))
        kvm4 = kvm.reshape(1, batch, n_mem, 2 * xd)
        qx = _norm_matmul(yp, norm_g[l, 3], wxq[l], BF16)
        ox = _cross_attention(qx, kvm4, kvm4, 0, 1, 0, batch, seq, x_heads, x_hd, BF16)
        yp = _matmul_residual(ox, wxo[l], yp)
        qx = _norm_matmul(ys, norm_g[l, 3], wxq[l], F32)
        ox = _cross_attention(qx, cache_mem_k.reshape(depth, n_seq, n_mem * x_heads, x_hd),
                              cache_mem_v.reshape(depth, n_seq, n_mem * x_heads, x_hd),
                              0, 0, l, n_seq, n_tok, x_heads, x_hd, F32, head_rows=True)
        ys = _matmul_residual(ox, wxo[l], ys)
        yp = _ffn(yp, norm_g[l, 4], wg[l, 1], wu[l, 1], wd[l, 1])
        ys = _ffn(ys, norm_g[l, 4], wg[l, 1], wu[l, 1], wd[l, 1])

    y_prompt = _final_norm(yp, final_g).reshape(batch, seq, d)
    y_sample = _final_norm(ys, final_g).reshape(n_seq, n_tok, d)

    def stack(entries, i):
        return jnp.stack([e[i] for e in entries])

    return (y_prompt, y_sample,
            stack(fox_p, 0), stack(fox_p, 1), stack(fox_p, 2),
            stack(swa_p, 0), stack(swa_p, 1),
            stack(ssm_p, 0), stack(ssm_p, 1),
            stack(mem_p, 0), stack(mem_p, 1),
            stack(fox_s, 0), stack(fox_s, 1), stack(fox_s, 2),
            stack(swa_s, 0), stack(swa_s, 1),
            stack(ssm_s, 0), stack(ssm_s, 1))
```
